```python
import math
import jax, jax.numpy as jnp
from jax import lax
import numpy as np

D_MODEL = 1024
BATCH = 2
SEQ = 8192
DEPTH = 2
DEC_BATCH = 32
DEC_SEQ = 1
PAST_LEN = 8192
PAGE_SIZE = 128

S5_WIDTH = D_MODEL // 2
S5_GROUP = 16
S5_GROUPS = S5_WIDTH // S5_GROUP
S5_STATE = 64
RW_HEAD_DIM = 64
RW_WIDTH = D_MODEL // 2
RW_HEADS = RW_WIDTH // RW_HEAD_DIM
RW_DECAY_RANK = 64
RW_AICL_RANK = 64
RW_GATE_RANK = 128
RW_PROJ = 3 * RW_WIDTH + RW_DECAY_RANK + RW_AICL_RANK + RW_GATE_RANK
RW_GN_EPS = 64e-5
SB_HEAD_DIM = 64
SB_WIDTH = D_MODEL // 2
SB_HEADS = SB_WIDTH // SB_HEAD_DIM
SB_BIAS_INIT = -6.0
Q_BLOCK = 128
N_BRANCH = 3
MIX_WIDTH = S5_WIDTH + RW_WIDTH + SB_WIDTH
N_IN = S5_WIDTH + RW_PROJ + 3 * SB_WIDTH + N_BRANCH * D_MODEL
D_FF = ((8 * D_MODEL // 3 + 127) // 128) * 128
N_SUB = 3
HALF_STEP = 0.5
RMS_EPS = 1e-6

kernel_name = 'hybrid_s5_rwkv7_stickbreak_decoder'


def _rmsnorm(x, g):
    xf = x.astype(jnp.float32)
    y = xf * lax.rsqrt(jnp.mean(xf * xf, axis=-1, keepdims=True) + RMS_EPS)
    return (y * g.astype(jnp.float32)).astype(x.dtype)


def _swiglu(h, w_up, w_down):
    gu = h @ w_up
    g, u = gu[..., :D_FF], gu[..., D_FF:]
    return (jax.nn.silu(g) * u) @ w_down


def _s5_branch(u, lam_re, lam_im, log_dt, b_re, b_im, c_re, c_im, d_skip, w_glu, b_glu, x0_re, x0_im):
    bsz, L, _ = u.shape
    f32 = jnp.float32
    ug = u.reshape(bsz, L, S5_GROUPS, S5_GROUP).astype(f32)
    lr, li = lam_re.astype(f32), lam_im.astype(f32)
    dt = jnp.exp(log_dt.astype(f32))[:, None]
    mag = jnp.exp(lr * dt)
    ab_re, ab_im = mag * jnp.cos(li * dt), mag * jnp.sin(li * dt)
    den = lr * lr + li * li
    nr, ni = ab_re - 1.0, ab_im
    f_re = (nr * lr + ni * li) / den
    f_im = (ni * lr - nr * li) / den
    br, bi = b_re.astype(f32), b_im.astype(f32)
    bb_re = f_re[..., None] * br - f_im[..., None] * bi
    bb_im = f_re[..., None] * bi + f_im[..., None] * br
    bu_re = jnp.einsum('blgc,gpc->blgp', ug, bb_re)
    bu_im = jnp.einsum('blgc,gpc->blgp', ug, bb_im)
    x0r, x0i = x0_re.astype(f32), x0_im.astype(f32)
    bu_re = bu_re.at[:, 0].add(ab_re * x0r - ab_im * x0i)
    bu_im = bu_im.at[:, 0].add(ab_re * x0i + ab_im * x0r)
    a_re = jnp.broadcast_to(ab_re, bu_re.shape)
    a_im = jnp.broadcast_to(ab_im, bu_im.shape)

    def combine(e1, e2):
        a1r, a1i, b1r, b1i = e1
        a2r, a2i, b2r, b2i = e2
        return (a1r * a2r - a1i * a2i, a1r * a2i + a1i * a2r,
                a2r * b1r - a2i * b1i + b2r, a2r * b1i + a2i * b1r + b2i)

    _, _, xr, xi = lax.associative_scan(combine, (a_re, a_im, bu_re, bu_im), axis=1)
    y = (jnp.einsum('gcp,blgp->blgc', c_re.astype(f32), xr)
         - jnp.einsum('gcp,blgp->blgc', c_im.astype(f32), xi)
         + d_skip.astype(f32) * ug).reshape(bsz, L, S5_WIDTH)
    z = jax.nn.gelu(y)
    out = z * jax.nn.sigmoid(z @ w_glu.astype(f32) + b_glu.astype(f32))
    return out.astype(u.dtype), xr[:, -1], xi[:, -1]


def _rwkv_branch(p, shift0, s0, mu, w0, w2, a0, a2, g2, k_k, k_a, r_k, gn_w, gn_b):
    bsz, L, _ = p.shape
    f32 = jnp.float32
    p_prev = jnp.concatenate([shift0[:, None, :].astype(p.dtype), p[:, :-1]], axis=1)
    ps = (p + (p_prev - p) * mu).astype(f32)
    o1, o2, o3 = RW_WIDTH, 2 * RW_WIDTH, 3 * RW_WIDTH
    o4 = o3 + RW_DECAY_RANK
    o5 = o4 + RW_AICL_RANK
    r, k, v = ps[..., :o1], ps[..., o1:o2], ps[..., o2:o3]
    dw, da, dg = ps[..., o3:o4], ps[..., o4:o5], ps[..., o5:]
    w_log = -jax.nn.softplus(-(w0.astype(f32) + jnp.tanh(dw) @ w2.astype(f32))) - 0.5
    decay = jnp.exp(-jnp.exp(w_log))
    a = jax.nn.sigmoid(a0.astype(f32) + da @ a2.astype(f32))
    g = jax.nn.sigmoid(dg) @ g2.astype(f32)

    def heads(t):
        return t.reshape(bsz, L, RW_HEADS, RW_HEAD_DIM)

    kk = heads(k * k_k.astype(f32))
    kk = kk / jnp.maximum(jnp.linalg.norm(kk, axis=-1, keepdims=True), 1e-12)
    k = k * (1.0 + (a - 1.0) * k_a.astype(f32))
    r, k, v, decay, a = heads(r), heads(k), heads(v), heads(decay), heads(a)

    def step(S, inp):
        r_t, w_t, k_t, v_t, kk_t, a_t = inp
        sa = jnp.einsum('bhvk,bhk->bhv', S, -kk_t)
        S = (S * w_t[:, :, None, :] + sa[..., None] * (kk_t * a_t)[:, :, None, :]
             + v_t[..., None] * k_t[:, :, None, :])
        return S, jnp.einsum('bhvk,bhk->bhv', S, r_t)

    def tm(t):
        return jnp.swapaxes(t, 0, 1)

    s_fin, o = lax.scan(step, s0.astype(f32), (tm(r), tm(decay), tm(k), tm(v), tm(kk), tm(a)))
    o = tm(o)
    mean = jnp.mean(o, axis=-1, keepdims=True)
    var = jnp.mean(jnp.square(o - mean), axis=-1, keepdims=True)
    o = ((o - mean) * lax.rsqrt(var + RW_GN_EPS) * gn_w.astype(f32).reshape(RW_HEADS, RW_HEAD_DIM)
         + gn_b.astype(f32).reshape(RW_HEADS, RW_HEAD_DIM))
    o = o + jnp.sum(r * k * r_k.astype(f32), axis=-1, keepdims=True) * v
    o = o.reshape(bsz, L, RW_WIDTH) * g
    return o.astype(p.dtype), s_fin, p[:, -1]


def _sb_block(qb, qpos, k, v, kpos, bias):
    z = jnp.einsum('bqhd,bkhd->bhqk', qb, k) * (SB_HEAD_DIM ** -0.5) + bias[None, :, None, None]
    mask = kpos[None, :] < qpos[:, None]
    log_keep = jnp.where(mask, -jax.nn.softplus(z), 0.0)
    suffix = lax.cumsum(log_keep, axis=3, reverse=True)
    log_w = jax.nn.log_sigmoid(z) + suffix - log_keep
    w = jnp.where(mask, jnp.exp(log_w), 0.0)
    return jnp.einsum('bhqk,bkhd->bqhd', w, v)


def _stick_breaking(q, k, v, q_off, bias):
    bsz, Lq = q.shape[0], q.shape[1]
    Lk = k.shape[1]
    f32 = jnp.float32
    kf, vf = k.astype(f32), v.astype(f32)
    bf = bias.astype(f32)
    kpos = jnp.arange(Lk)
    qpos = q_off + jnp.arange(Lq)
    blk = Q_BLOCK if Lq % Q_BLOCK == 0 else Lq
    nb = Lq // blk
    qb = q.astype(f32).reshape(bsz, nb, blk, SB_HEADS, SB_HEAD_DIM).transpose(1, 0, 2, 3, 4)
    pb = qpos.reshape(nb, blk)
    o = lax.map(lambda xs: _sb_block(xs[0], xs[1], kf, vf, kpos, bf), (qb, pb))
    o = o.transpose(1, 0, 2, 3, 4).reshape(bsz, Lq, SB_WIDTH)
    return o.astype(q.dtype)


def _layer(x, c, P, past_k, past_v, s5_re0, s5_im0, rw_s0, rw_shift0):
    bsz, L, _ = x.shape
    mod = (jax.nn.silu(c) @ P['w_ada'] + P['b_ada']).reshape(bsz, N_SUB, 3, 1, D_MODEL)

    def normed(x, i):
        return _rmsnorm(x, P['norm_g'][i]) * (1.0 + mod[:, i, 1]) + mod[:, i, 0]

    x = x + HALF_STEP * mod[:, 0, 2] * _swiglu(normed(x, 0), P['w_ffn_up'][0], P['w_ffn_down'][0])
    h = normed(x, 1)
    proj = h @ P['w_in']
    e0 = S5_WIDTH
    e1 = e0 + RW_PROJ
    e2 = e1 + SB_WIDTH
    e3 = e2 + SB_WIDTH
    e4 = e3 + SB_WIDTH
    y_s5, s5_re, s5_im = _s5_branch(proj[..., :e0], P['s5_lam_re'], P['s5_lam_im'], P['s5_log_dt'],
                                    P['s5_b_re'], P['s5_b_im'], P['s5_c_re'], P['s5_c_im'], P['s5_d'],
                                    P['s5_w_glu'], P['s5_b_glu'], s5_re0, s5_im0)
    y_rw, rw_s, rw_shift = _rwkv_branch(proj[..., e0:e1], rw_shift0, rw_s0, P['rw_mu'], P['rw_w0'],
                                        P['rw_w2'], P['rw_a0'], P['rw_a2'], P['rw_g2'], P['rw_k_k'],
                                        P['rw_k_a'], P['rw_r_k'], P['rw_gn_w'], P['rw_gn_b'])

    def heads(t):
        return t.reshape(bsz, L, SB_HEADS, SB_HEAD_DIM)

    q, k, v = heads(proj[..., e1:e2]), heads(proj[..., e2:e3]), heads(proj[..., e3:e4])
    if past_k is None:
        k_all, v_all, off = k, v, 0
    else:
        k_all = jnp.concatenate([past_k.astype(k.dtype), k], axis=1)
        v_all = jnp.concatenate([past_v.astype(v.dtype), v], axis=1)
        off = past_k.shape[1]
    y_sb = _stick_breaking(q, k_all, v_all, off, P['sb_bias'])
    gates = jax.nn.sigmoid(proj[..., e4:]).reshape(bsz, L, N_BRANCH, D_MODEL)
    wb = P['w_branch']
    r0 = S5_WIDTH
    r1 = S5_WIDTH + RW_WIDTH
    merged = (gates[:, :, 0] * (y_s5 @ wb[:r0])
              + gates[:, :, 1] * (y_rw @ wb[r0:r1])
              + gates[:, :, 2] * (y_sb @ wb[r1:]))
    x = x + mod[:, 1, 2] * (merged @ P['w_out'])
    x = x + HALF_STEP * mod[:, 2, 2] * _swiglu(normed(x, 2), P['w_ffn_up'][1], P['w_ffn_down'][1])
    return x, (k, v, s5_re, s5_im, rw_s, rw_shift)


def setup_inputs(seed: int = 0) -> dict:
    key = jax.random.key(seed)
    ks = iter(jax.random.split(key, 64))
    f32 = jnp.float32

    def nrm(shape, scale):
        return jax.random.normal(next(ks), shape, f32) * scale

    n_pages = PAST_LEN // PAGE_SIZE
    n_used = DEC_BATCH * n_pages
    n_pool = n_used + max(1, n_used // 4)
    x_prompt = nrm((BATCH, SEQ, D_MODEL), 1.0)
    x_sample = nrm((DEC_BATCH, DEC_SEQ, D_MODEL), 1.0)
    cache_k = nrm((DEPTH, n_pool, PAGE_SIZE, SB_HEADS, SB_HEAD_DIM), 1.0)
    cache_v = nrm((DEPTH, n_pool, PAGE_SIZE, SB_HEADS, SB_HEAD_DIM), 1.0)
    state_s5_re = nrm((DEPTH, DEC_BATCH, S5_GROUPS, S5_STATE), 0.1)
    state_s5_im = nrm((DEPTH, DEC_BATCH, S5_GROUPS, S5_STATE), 0.1)
    state_rwkv = nrm((DEPTH, DEC_BATCH, RW_HEADS, RW_HEAD_DIM, RW_HEAD_DIM), 0.1)
    state_rwkv_shift = nrm((DEPTH, DEC_BATCH, RW_PROJ), 1.0)
    page_table = jax.random.permutation(next(ks), n_pool)[:n_used].reshape(DEC_BATCH, n_pages).astype(jnp.int32)
    c_prompt = nrm((BATCH, D_MODEL), 1.0)
    c_sample = nrm((DEC_BATCH, D_MODEL), 1.0)
    norm_g = 1.0 + nrm((DEPTH, N_SUB, D_MODEL), 0.02)
    w_ada = nrm((DEPTH, D_MODEL, N_SUB * 3 * D_MODEL), 0.5 * D_MODEL ** -0.5)
    b_ada = nrm((DEPTH, N_SUB * 3 * D_MODEL), 0.02)
    w_ffn_up = nrm((DEPTH, 2, D_MODEL, 2 * D_FF), D_MODEL ** -0.5)
    w_ffn_down = nrm((DEPTH, 2, D_FF, D_MODEL), D_FF ** -0.5)
    w_in = nrm((DEPTH, D_MODEL, N_IN), D_MODEL ** -0.5)
    n_idx = jnp.arange(S5_STATE, dtype=f32)
    s5_lam_re = -0.5 + nrm((DEPTH, S5_GROUPS, S5_STATE), 0.01)
    s5_lam_im = math.pi * n_idx + nrm((DEPTH, S5_GROUPS, S5_STATE), 0.01)
    s5_log_dt = jax.random.uniform(next(ks), (DEPTH, S5_GROUPS), f32, math.log(1e-3), math.log(1e-1))
    s5_b_re = nrm((DEPTH, S5_GROUPS, S5_STATE, S5_GROUP), (2 * S5_GROUP) ** -0.5)
    s5_b_im = nrm((DEPTH, S5_GROUPS, S5_STATE, S5_GROUP), (2 * S5_GROUP) ** -0.5)
    s5_c_re = nrm((DEPTH, S5_GROUPS, S5_GROUP, S5_STATE), S5_STATE ** -0.5)
    s5_c_im = nrm((DEPTH, S5_GROUPS, S5_GROUP, S5_STATE), S5_STATE ** -0.5)
    s5_d = nrm((DEPTH, S5_GROUPS, S5_GROUP), 1.0)
    s5_w_glu = nrm((DEPTH, S5_WIDTH, S5_WIDTH), S5_WIDTH ** -0.5)
    s5_b_glu = nrm((DEPTH, S5_WIDTH), 0.02)
    rw_mu = jax.random.uniform(next(ks), (DEPTH, RW_PROJ), f32)
    ratio = jnp.linspace(0.0, 1.0, RW_WIDTH, dtype=f32)
    rw_w0 = (-7.0 + 5.0 * ratio ** 0.85 + 0.5) + nrm((DEPTH, RW_WIDTH), 0.1)
    rw_w2 = nrm((DEPTH, RW_DECAY_RANK, RW_WIDTH), 0.1 * RW_DECAY_RANK ** -0.5)
    rw_a0 = nrm((DEPTH, RW_WIDTH), 0.1)
    rw_a2 = nrm((DEPTH, RW_AICL_RANK, RW_WIDTH), 0.3 * RW_AICL_RANK ** -0.5)
    rw_g2 = nrm((DEPTH, RW_GATE_RANK, RW_WIDTH), RW_GATE_RANK ** -0.5)
    rw_k_k = 0.85 + nrm((DEPTH, RW_WIDTH), 0.02)
    rw_k_a = 1.0 + nrm((DEPTH, RW_WIDTH), 0.02)
    rw_r_k = nrm((DEPTH, RW_HEADS, RW_HEAD_DIM), 0.1)
    rw_gn_w = 1.0 + nrm((DEPTH, RW_WIDTH), 0.02)
    rw_gn_b = nrm((DEPTH, RW_WIDTH), 0.02)
    sb_bias = SB_BIAS_INIT + nrm((DEPTH, SB_HEADS), 0.1)
    w_branch = nrm((DEPTH, MIX_WIDTH, D_MODEL), (D_MODEL // 2) ** -0.5)
    w_out = nrm((DEPTH, D_MODEL, D_MODEL), D_MODEL ** -0.5)
    norm_f = 1.0 + nrm((D_MODEL,), 0.02)
    return {'x_prompt': x_prompt, 'x_sample': x_sample, 'cache_k': cache_k, 'cache_v': cache_v,
            'state_s5_re': state_s5_re, 'state_s5_im': state_s5_im, 'state_rwkv': state_rwkv,
            'state_rwkv_shift': state_rwkv_shift, 'page_table': page_table,
            'c_prompt': c_prompt, 'c_sample': c_sample,
            'norm_g': norm_g, 'w_ada': w_ada, 'b_ada': b_ada, 'w_ffn_up': w_ffn_up, 'w_ffn_down': w_ffn_down,
            'w_in': w_in, 's5_lam_re': s5_lam_re, 's5_lam_im': s5_lam_im, 's5_log_dt': s5_log_dt,
            's5_b_re': s5_b_re, 's5_b_im': s5_b_im, 's5_c_re': s5_c_re, 's5_c_im': s5_c_im, 's5_d': s5_d,
            's5_w_glu': s5_w_glu, 's5_b_glu': s5_b_glu, 'rw_mu': rw_mu, 'rw_w0': rw_w0, 'rw_w2': rw_w2,
            'rw_a0': rw_a0, 'rw_a2': rw_a2, 'rw_g2': rw_g2, 'rw_k_k': rw_k_k, 'rw_k_a': rw_k_a,
            'rw_r_k': rw_r_k, 'rw_gn_w': rw_gn_w, 'rw_gn_b': rw_gn_b, 'sb_bias': sb_bias,
            'w_branch': w_branch, 'w_out': w_out, 'norm_f': norm_f}


def reference(x_prompt, x_sample, cache_k, cache_v, state_s5_re, state_s5_im, state_rwkv, state_rwkv_shift,
              page_table, c_prompt, c_sample, norm_g, w_ada, b_ada, w_ffn_up, w_ffn_down, w_in,
              s5_lam_re, s5_lam_im, s5_log_dt, s5_b_re, s5_b_im, s5_c_re, s5_c_im, s5_d, s5_w_glu, s5_b_glu,
              rw_mu, rw_w0, rw_w2, rw_a0, rw_a2, rw_g2, rw_k_k, rw_k_a, rw_r_k, rw_gn_w, rw_gn_b,
              sb_bias, w_branch, w_out, norm_f):
    bp, bs = x_prompt.shape[0], x_sample.shape[0]
    n_pages = page_table.shape[1]
    past_len = n_pages * PAGE_SIZE
    xp, xs = x_prompt, x_sample
    kp, vp, s5rp, s5ip, rwp, shp = [], [], [], [], [], []
    ksm, vsm, s5rs, s5is, rws, shs = [], [], [], [], [], []
    for l in range(DEPTH):
        P = dict(norm_g=norm_g[l], w_ada=w_ada[l], b_ada=b_ada[l], w_ffn_up=w_ffn_up[l],
                 w_ffn_down=w_ffn_down[l], w_in=w_in[l], s5_lam_re=s5_lam_re[l], s5_lam_im=s5_lam_im[l],
                 s5_log_dt=s5_log_dt[l], s5_b_re=s5_b_re[l], s5_b_im=s5_b_im[l], s5_c_re=s5_c_re[l],
                 s5_c_im=s5_c_im[l], s5_d=s5_d[l], s5_w_glu=s5_w_glu[l], s5_b_glu=s5_b_glu[l],
                 rw_mu=rw_mu[l], rw_w0=rw_w0[l], rw_w2=rw_w2[l], rw_a0=rw_a0[l], rw_a2=rw_a2[l],
                 rw_g2=rw_g2[l], rw_k_k=rw_k_k[l], rw_k_a=rw_k_a[l], rw_r_k=rw_r_k[l],
                 rw_gn_w=rw_gn_w[l], rw_gn_b=rw_gn_b[l], sb_bias=sb_bias[l],
                 w_branch=w_branch[l], w_out=w_out[l])
        z_s5 = jnp.zeros((bp, S5_GROUPS, S5_STATE), x_prompt.dtype)
        z_rw = jnp.zeros((bp, RW_HEADS, RW_HEAD_DIM, RW_HEAD_DIM), x_prompt.dtype)
        z_sh = jnp.zeros((bp, RW_PROJ), x_prompt.dtype)
        xp, st = _layer(xp, c_prompt, P, None, None, z_s5, z_s5, z_rw, z_sh)
        kp.append(st[0]); vp.append(st[1]); s5rp.append(st[2]); s5ip.append(st[3])
        rwp.append(st[4]); shp.append(st[5])
        past_k = cache_k[l][page_table].reshape(bs, past_len, SB_HEADS, SB_HEAD_DIM)
        past_v = cache_v[l][page_table].reshape(bs, past_len, SB_HEADS, SB_HEAD_DIM)
        xs, st = _layer(xs, c_sample, P, past_k, past_v, state_s5_re[l], state_s5_im[l],
                        state_rwkv[l], state_rwkv_shift[l])
        ksm.append(st[0]); vsm.append(st[1]); s5rs.append(st[2]); s5is.append(st[3])
        rws.append(st[4]); shs.append(st[5])
    y_prompt = _rmsnorm(xp, norm_f)
    y_sample = _rmsnorm(xs, norm_f)
    return (y_prompt, y_sample,
            jnp.stack(kp), jnp.stack(vp), jnp.stack(s5rp), jnp.stack(s5ip), jnp.stack(rwp), jnp.stack(shp),
            jnp.stack(ksm), jnp.stack(vsm), jnp.stack(s5rs), jnp.stack(s5is), jnp.stack(rws), jnp.stack(shs))
```

```python
import functools
import math

import jax
import jax.numpy as jnp
from jax import lax
from jax.experimental import pallas as pl
from jax.experimental.pallas import tpu as pltpu

F32 = jnp.float32
BF16 = jnp.bfloat16

S5_GROUP = 16
S5_STATE = 64
HEAD_DIM = 64
RW_DECAY_RANK = 64
RW_AICL_RANK = 64
RW_GATE_RANK = 128
RW_GN_EPS = 64e-5
HALF_STEP = 0.5
RMS_EPS = 1e-6
PAGE_SIZE = 128

LANES = 128
SUBLANES = 8
MXU_WIDTH = 256
VMEM_LIMIT = 52 * 1024 * 1024


def _params(sem):
    return pltpu.CompilerParams(dimension_semantics=sem, vmem_limit_bytes=VMEM_LIMIT)


def _const_spec(shape):
    nd = len(shape)
    return pl.BlockSpec(shape, lambda *_: (0,) * nd, pipeline_mode=pl.Buffered(1))


def _dot(a, b):
    return jnp.dot(a.astype(BF16), b.astype(BF16), preferred_element_type=F32)


def _dot_nt(a, b):
    return lax.dot_general(a.astype(BF16), b.astype(BF16), (((1,), (1,)), ((), ())),
                           preferred_element_type=F32)


def _dot_tn(a, b):
    return lax.dot_general(a.astype(BF16), b.astype(BF16), (((0,), (0,)), ((), ())),
                           preferred_element_type=F32)


def _split(a):
    hi = a.astype(BF16)
    lo = (a - hi.astype(F32)).astype(BF16)
    return hi, lo


def _dot_hl(a, b01):
    hi, lo = _split(a)
    return (jnp.dot(hi, b01, preferred_element_type=F32)
            + jnp.dot(lo, b01, preferred_element_type=F32))


def _dot_lh(a01, b):
    hi, lo = _split(b)
    return (jnp.dot(a01, hi, preferred_element_type=F32)
            + jnp.dot(a01, lo, preferred_element_type=F32))


def _softplus(x):
    return jnp.maximum(x, 0.0) + jnp.log1p(jnp.exp(-jnp.abs(x)))


def _norm_mod(x, g, scale, shift):
    ms = jnp.mean(x * x, axis=-1, keepdims=True)
    return (x * lax.rsqrt(ms + RMS_EPS) * g) * (1.0 + scale) + shift


def _row_tile(rows, want):
    return want if rows % want == 0 else rows


def _mod_spec(mod, tm):
    _, rm, d = mod.shape
    if rm == 1:
        return pl.BlockSpec((1, 1, d), lambda g, i: (g, 0, 0))
    return pl.BlockSpec((1, tm, d), lambda g, i: (g, i, 0))


def _ada_kernel(c_ref, w_ref, b_ref, o_ref):
    c = c_ref[...]
    o_ref[0] = _dot(c * jax.nn.sigmoid(c), w_ref[0]) + b_ref[0]


def _ada(c_all, w_ada, b_ada):
    depth, d, n = w_ada.shape
    rows = c_all.shape[0]
    tn = 1024
    return pl.pallas_call(
        _ada_kernel,
        grid=(depth, n // tn),
        in_specs=[pl.BlockSpec((rows, d), lambda l, j: (0, 0)),
                  pl.BlockSpec((1, d, tn), lambda l, j: (l, 0, j)),
                  pl.BlockSpec((1, 1, tn), lambda l, j: (l, 0, j))],
        out_specs=pl.BlockSpec((1, rows, tn), lambda l, j: (l, 0, j)),
        out_shape=jax.ShapeDtypeStruct((depth, rows, n), F32),
        compiler_params=_params(("parallel", "parallel")),
        name="adaln",
    )(c_all, w_ada, b_ada.reshape(depth, 1, n))


def _ffn_kernel(x_ref, sh_ref, sc_ref, gt_ref, g_ref, wup_ref, wdn_ref, *rest, dff, chunk, final):
    if final:
        nf_ref, o_ref, acc_ref = rest
    else:
        o_ref, acc_ref = rest
    x = x_ref[0]
    h = _norm_mod(x, g_ref[...], sc_ref[0], sh_ref[0]).astype(BF16)
    for j in range(dff // chunk):
        c0, c1 = j * chunk, (j + 1) * chunk
        g = jnp.dot(h, wup_ref[:, c0:c1], preferred_element_type=F32)
        u = jnp.dot(h, wup_ref[:, dff + c0:dff + c1], preferred_element_type=F32)
        act = (g * jax.nn.sigmoid(g) * u).astype(BF16)
        d = jnp.dot(act, wdn_ref[c0:c1, :], preferred_element_type=F32)
        if j == 0:
            acc_ref[...] = d
        else:
            acc_ref[...] += d
    y = x + HALF_STEP * gt_ref[0] * acc_ref[...]
    if final:
        ms = jnp.mean(y * y, axis=-1, keepdims=True)
        y = y * lax.rsqrt(ms + RMS_EPS) * nf_ref[...]
    o_ref[0] = y


def _ffn(x, shift, scale, gate, g, w_up, w_down, norm_f=None):
    G, R, D = x.shape
    dff = w_down.shape[0]
    tm = _row_tile(R, 512)
    final = norm_f is not None
    in_specs = [pl.BlockSpec((1, tm, D), lambda gi, i: (gi, i, 0)),
                _mod_spec(shift, tm), _mod_spec(scale, tm), _mod_spec(gate, tm),
                _const_spec((1, D)), _const_spec(w_up.shape), _const_spec(w_down.shape)]
    args = [x, shift, scale, gate, g.reshape(1, D), w_up, w_down]
    if final:
        in_specs.append(_const_spec((1, D)))
        args.append(norm_f.reshape(1, D))
    return pl.pallas_call(
        functools.partial(_ffn_kernel, dff=dff, chunk=MXU_WIDTH, final=final),
        grid=(G, R // tm),
        in_specs=in_specs,
        out_specs=pl.BlockSpec((1, tm, D), lambda gi, i: (gi, i, 0)),
        out_shape=jax.ShapeDtypeStruct((G, R, D), F32),
        scratch_shapes=[pltpu.VMEM((tm, D), F32)],
        compiler_params=_params(("parallel", "parallel")),
        name="ffn",
    )(*args)


def _inproj_kernel(x_ref, sh_ref, sc_ref, g_ref, w_ref, *o_refs, widths, chunk):
    h = _norm_mod(x_ref[0], g_ref[...], sc_ref[0], sh_ref[0]).astype(BF16)
    off = 0
    for o_ref, wd in zip(o_refs, widths):
        for c0 in range(0, wd, chunk):
            o_ref[0, :, c0:c0 + chunk] = jnp.dot(h, w_ref[:, off + c0:off + c0 + chunk],
                                                 preferred_element_type=F32)
        off += wd


def _inproj(x, shift, scale, g, w_in, widths):
    G, R, D = x.shape
    tm = _row_tile(R, 256)
    return pl.pallas_call(
        functools.partial(_inproj_kernel, widths=widths, chunk=MXU_WIDTH),
        grid=(G, R // tm),
        in_specs=[pl.BlockSpec((1, tm, D), lambda gi, i: (gi, i, 0)),
                  _mod_spec(shift, tm), _mod_spec(scale, tm),
                  _const_spec((1, D)), _const_spec(w_in.shape)],
        out_specs=[pl.BlockSpec((1, tm, wd), lambda gi, i: (gi, i, 0)) for wd in widths],
        out_shape=[jax.ShapeDtypeStruct((G, R, wd), F32) for wd in widths],
        compiler_params=_params(("parallel", "parallel")),
        name="inproj",
    )(x, shift, scale, g.reshape(1, D), w_in)


def _merge_kernel(x_ref, gt_ref, y0_ref, y1_ref, y2_ref, gates_ref, wb_ref, wo_ref, o_ref, *, d, bw):
    merged = None
    for i, y_ref in enumerate((y0_ref, y1_ref, y2_ref)):
        t = jax.nn.sigmoid(gates_ref[0, :, i * d:(i + 1) * d]) * _dot(y_ref[0], wb_ref[i * bw:(i + 1) * bw, :])
        merged = t if merged is None else merged + t
    o_ref[0] = x_ref[0] + gt_ref[0] * _dot(merged, wo_ref[...])


def _merge(x, gate, y_s5, y_rw, y_sb, gates, w_branch, w_out):
    G, R, D = x.shape
    bw = y_s5.shape[-1]
    tm = _row_tile(R, 512)
    row = lambda wd: pl.BlockSpec((1, tm, wd), lambda gi, i: (gi, i, 0))
    return pl.pallas_call(
        functools.partial(_merge_kernel, d=D, bw=bw),
        grid=(G, R // tm),
        in_specs=[row(D), _mod_spec(gate, tm), row(bw), row(bw), row(bw), row(3 * D),
                  _const_spec(w_branch.shape), _const_spec(w_out.shape)],
        out_specs=row(D),
        out_shape=jax.ShapeDtypeStruct((G, R, D), F32),
        compiler_params=_params(("parallel", "parallel")),
        name="merge",
    )(x, gate, y_s5, y_rw, y_sb, gates, w_branch, w_out)


S5_POWERS = SUBLANES


def _s5_param_kernel(lr_ref, li_ref, ldt_ref, br_ref, bi_ref, bbr_ref, bbi_ref, pwr_ref, pwi_ref):
    lr, li = lr_ref[...], li_ref[...]
    dt = jnp.exp(ldt_ref[...])
    mag = jnp.exp(lr * dt)
    ab_re, ab_im = mag * jnp.cos(li * dt), mag * jnp.sin(li * dt)
    den = lr * lr + li * li
    nr, ni = ab_re - 1.0, ab_im
    f_re = (nr * lr + ni * li) / den
    f_im = (ni * lr - nr * li) / den
    br, bi = br_ref[...], bi_ref[...]
    bbr_ref[...] = f_re * br - f_im * bi
    bbi_ref[...] = f_re * bi + f_im * br
    pr, pi = ab_re, ab_im
    for j in range(S5_POWERS):
        pwr_ref[j] = pr
        pwi_ref[j] = pi
        pr, pi = pr * ab_re - pi * ab_im, pr * ab_im + pi * ab_re


def _s5_params(lam_re, lam_im, log_dt, b_re, b_im):
    G, P = lam_re.shape
    C = b_re.shape[-1]
    sds = jax.ShapeDtypeStruct
    return pl.pallas_call(
        _s5_param_kernel,
        out_shape=[sds((G, C, P), F32), sds((G, C, P), F32),
                   sds((S5_POWERS, G, 1, P), F32), sds((S5_POWERS, G, 1, P), F32)],
        name="s5_params",
    )(lam_re.reshape(G, 1, P), lam_im.reshape(G, 1, P),
      jnp.broadcast_to(log_dt[:, None, None], (G, 1, P)),
      jnp.swapaxes(b_re, 1, 2), jnp.swapaxes(b_im, 1, 2))


S5_LANE_CHUNK = 256


def _s5_kernel(u_ref, x0r_ref, x0i_ref, bbr_ref, bbi_ref, cr_ref, ci_ref, d_ref, wg_ref, bg_ref,
               pwr_ref, pwi_ref, y_ref, xr_out, xi_out, sr_ref, si_ref, car_ref, cai_ref,
               *, tl, nl, last_row):
    li = pl.program_id(1)

    @pl.when(li == 0)
    def _():
        car_ref[...] = x0r_ref[0]
        cai_ref[...] = x0i_ref[0]

    u = u_ref[0]
    ub = u.astype(BF16)
    sr_ref[...] = jnp.dot(ub, bbr_ref[...], preferred_element_type=F32)
    si_ref[...] = jnp.dot(ub, bbi_ref[...], preferred_element_type=F32)
    n_state = sr_ref.shape[1]
    lc = S5_LANE_CHUNK
    row = lax.broadcasted_iota(jnp.int32, (SUBLANES, lc), 0)
    for c in range(n_state // lc):
        ls = slice(c * lc, (c + 1) * lc)
        pr8, pi8 = pwr_ref[:, ls], pwi_ref[:, ls]
        steps = []
        for dd in (1, 2, 4):
            steps.append((dd, jnp.where(row >= dd, pr8[dd - 1:dd, :], 0.0),
                          jnp.where(row >= dd, pi8[dd - 1:dd, :], 0.0)))

        def body(i, carry, ls=ls, pr8=pr8, pi8=pi8, steps=steps):
            car, cai = carry
            r0 = pl.multiple_of(i * SUBLANES, SUBLANES)
            xr = sr_ref[pl.ds(r0, SUBLANES), ls]
            xi = si_ref[pl.ds(r0, SUBLANES), ls]
            for dd, ar, ai in steps:
                rr = pltpu.roll(xr, dd, 0)
                ri = pltpu.roll(xi, dd, 0)
                xr, xi = xr + ar * rr - ai * ri, xi + ar * ri + ai * rr
            xr, xi = xr + pr8 * car - pi8 * cai, xi + pr8 * cai + pi8 * car
            sr_ref[pl.ds(r0, SUBLANES), ls] = xr
            si_ref[pl.ds(r0, SUBLANES), ls] = xi
            return xr[SUBLANES - 1:SUBLANES, :], xi[SUBLANES - 1:SUBLANES, :]

        car, cai = lax.fori_loop(0, tl // SUBLANES, body, (car_ref[:, ls], cai_ref[:, ls]))
        car_ref[:, ls] = car
        cai_ref[:, ls] = cai

    y = (_dot(sr_ref[...], cr_ref[...]) - _dot(si_ref[...], ci_ref[...]) + d_ref[...] * u)
    z = jax.nn.gelu(y)
    y_ref[0] = z * jax.nn.sigmoid(_dot(z, wg_ref[...]) + bg_ref[...])

    @pl.when(li == nl - 1)
    def _():
        xr_out[0] = sr_ref[last_row:last_row + 1, :]
        xi_out[0] = si_ref[last_row:last_row + 1, :]


def _s5(u, x0_re, x0_im, prm, seq_len):
    B, Lp, W = u.shape
    n_state = prm["bb_re"].shape[1]
    tl = _row_tile(Lp, 256)
    nl = Lp // tl
    last_row = (seq_len - 1) - (nl - 1) * tl
    assert 0 <= last_row < tl
    state = pl.BlockSpec((1, 1, n_state), lambda b, i: (b, 0, 0))
    sds = jax.ShapeDtypeStruct
    return pl.pallas_call(
        functools.partial(_s5_kernel, tl=tl, nl=nl, last_row=last_row),
        grid=(B, nl),
        in_specs=[pl.BlockSpec((1, tl, W), lambda b, i: (b, i, 0)), state, state,
                  _const_spec(prm["bb_re"].shape), _const_spec(prm["bb_im"].shape),
                  _const_spec(prm["c_re"].shape), _const_spec(prm["c_im"].shape),
                  _const_spec((1, W)), _const_spec((W, W)), _const_spec((1, W)),
                  _const_spec((S5_POWERS, n_state)), _const_spec((S5_POWERS, n_state))],
        out_specs=[pl.BlockSpec((1, tl, W), lambda b, i: (b, i, 0)), state, state],
        out_shape=[sds((B, Lp, W), F32), sds((B, 1, n_state), F32), sds((B, 1, n_state), F32)],
        scratch_shapes=[pltpu.VMEM((tl, n_state), F32), pltpu.VMEM((tl, n_state), F32),
                        pltpu.VMEM((1, n_state), F32), pltpu.VMEM((1, n_state), F32)],
        compiler_params=_params(("parallel", "arbitrary")),
        name="s5",
    )(u, x0_re, x0_im, prm["bb_re"], prm["bb_im"], prm["c_re"], prm["c_im"], prm["d"],
      prm["w_glu"], prm["b_glu"], prm["pw_re"], prm["pw_im"])


def _rwkv_kernel(p_ref, sh0_ref, s0_ref, mu_ref, w0_ref, w2_ref, a0_ref, a2_ref, g2_ref, kk_ref, ka_ref,
                 rk_ref, gnw_ref, gnb_ref, seg_ref, tri_ref, y_ref, sfin_ref, S_ref, prev_ref,
                 *, T, nl, valid, heads):
    li = pl.program_id(1)
    W = heads * HEAD_DIM

    @pl.when(li == 0)
    def _():
        S_ref[...] = s0_ref[0]
        prev_ref[...] = sh0_ref[0]

    p = p_ref[0]
    trow = lax.broadcasted_iota(jnp.int32, (T, 1), 0)
    p_prev = jnp.where(trow == 0, prev_ref[...], pltpu.roll(p, 1, 0))
    prev_ref[...] = p[T - 1:T, :]
    ps = p + (p_prev - p) * mu_ref[...]
    r, k, v = ps[:, 0:W], ps[:, W:2 * W], ps[:, 2 * W:3 * W]
    o3 = 3 * W
    dw = ps[:, o3:o3 + RW_DECAY_RANK]
    da = ps[:, o3 + RW_DECAY_RANK:o3 + RW_DECAY_RANK + RW_AICL_RANK]
    dg = ps[:, o3 + RW_DECAY_RANK + RW_AICL_RANK:]
    w_log = -_softplus(-(w0_ref[...] + _dot(jnp.tanh(dw), w2_ref[...]))) - 0.5
    lw = -jnp.exp(w_log)
    a = jax.nn.sigmoid(a0_ref[...] + _dot(da, a2_ref[...]))
    g = _dot(jax.nn.sigmoid(dg), g2_ref[...])
    seg = seg_ref[...]
    kk = k * kk_ref[...]
    kk = kk / jnp.maximum(jnp.sqrt(_dot_hl(kk * kk, seg)), 1e-12)
    km = k * (1.0 + (a - 1.0) * ka_ref[...])
    bonus = _dot_hl(r * km * rk_ref[...], seg)
    if valid < T:
        live = trow < valid
        lw = jnp.where(live, lw, 0.0)
        kk = jnp.where(live, kk, 0.0)
        km = jnp.where(live, km, 0.0)

    cum = _dot_lh(tri_ref[...], lw)
    c_end = cum[T - 1:T, :]
    e_neg = jnp.exp(-cum)
    e_end = jnp.exp(c_end - cum)
    kka = kk * a
    A = -kk * jnp.exp(cum - lw)
    Bt = kka * e_neg
    Kt = km * e_neg
    Rt = r * jnp.exp(cum)
    Bh = kka * e_end
    Kh = km * e_end
    w_end = jnp.exp(c_end)

    ti = lax.broadcasted_iota(jnp.int32, (T, T), 0)
    tj = lax.broadcasted_iota(jnp.int32, (T, T), 1)
    strict = ti > tj
    incl = ti >= tj
    eye = jnp.where(ti == tj, 1.0, 0.0)
    for h in range(heads):
        hs = slice(h * HEAD_DIM, (h + 1) * HEAD_DIM)
        S = S_ref[h]
        AR = jnp.concatenate([A[:, hs], Rt[:, hs]], axis=0)
        BK = jnp.concatenate([Bt[:, hs], Kt[:, hs]], axis=0)
        Gm = _dot_nt(AR, BK)
        N = jnp.where(strict, Gm[0:T, 0:T], 0.0)
        M = jnp.where(strict, Gm[0:T, T:2 * T], 0.0)
        Prb = jnp.where(incl, Gm[T:2 * T, 0:T], 0.0)
        Prk = jnp.where(incl, Gm[T:2 * T, T:2 * T], 0.0)
        ARS = _dot_nt(AR, S)
        X = eye + N
        P = N
        n = 1
        while 2 * n < T:
            P = _dot(P, P)
            X = X + _dot(X, P)
            n *= 2
        vh = v[:, hs]
        U = _dot(X, ARS[0:T] + _dot(M, vh))
        O = ARS[T:2 * T] + _dot(Prb, U) + _dot(Prk, vh)
        S_ref[h] = S * w_end[:, hs] + _dot_tn(U, Bh[:, hs]) + _dot_tn(vh, Kh[:, hs])
        mean = jnp.mean(O, axis=-1, keepdims=True)
        var = jnp.mean(jnp.square(O - mean), axis=-1, keepdims=True)
        on = (O - mean) * lax.rsqrt(var + RW_GN_EPS) * gnw_ref[:, hs] + gnb_ref[:, hs]
        on = on + bonus[:, hs] * vh
        y_ref[0, :, hs] = on * g[:, hs]

    @pl.when(li == nl - 1)
    def _():
        sfin_ref[0] = S_ref[...]


def _rwkv(p, shift0, s0, prm, seq_len, T):
    B, Lp, NP = p.shape
    heads = s0.shape[1]
    W = heads * HEAD_DIM
    nl = Lp // T
    valid = seq_len - (nl - 1) * T
    assert nl * T == Lp and 0 < valid <= T and (nl == 1 or valid == T)
    tri = jnp.tril(jnp.ones((T, T), BF16))
    vec = lambda n: _const_spec((1, n))
    sds = jax.ShapeDtypeStruct
    st_spec = pl.BlockSpec((1, heads, HEAD_DIM, HEAD_DIM), lambda b, i: (b, 0, 0, 0))
    return pl.pallas_call(
        functools.partial(_rwkv_kernel, T=T, nl=nl, valid=valid, heads=heads),
        grid=(B, nl),
        in_specs=[pl.BlockSpec((1, T, NP), lambda b, i: (b, i, 0)),
                  pl.BlockSpec((1, 1, NP), lambda b, i: (b, 0, 0)), st_spec,
                  vec(NP), vec(W), _const_spec((RW_DECAY_RANK, W)), vec(W), _const_spec((RW_AICL_RANK, W)),
                  _const_spec((RW_GATE_RANK, W)), vec(W), vec(W), vec(W), vec(W), vec(W),
                  _const_spec((W, W)), _const_spec((T, T))],
        out_specs=[pl.BlockSpec((1, T, W), lambda b, i: (b, i, 0)), st_spec],
        out_shape=[sds((B, Lp, W), F32), sds(s0.shape, F32)],
        scratch_shapes=[pltpu.VMEM((heads, HEAD_DIM, HEAD_DIM), F32), pltpu.VMEM((1, NP), F32)],
        compiler_params=_params(("parallel", "arbitrary")),
        name="rwkv",
    )(p, shift0, s0, prm["mu"], prm["w0"], prm["w2"], prm["a0"], prm["a2"], prm["g2"], prm["k_k"],
      prm["k_a"], prm["r_k"], prm["gn_w"], prm["gn_b"], prm["seg"], tri)


def _sb_tile(qh, kt, vt, bias, upper, carry, mask):
    z = _dot_nt(qh, kt) + bias
    sp = jnp.maximum(z, 0.0) + jnp.log(1.0 + jnp.exp(-jnp.abs(z)))
    lk = -sp if mask is None else jnp.where(mask, -sp, 0.0)
    cs = _dot_hl(lk, upper) + carry
    w = jnp.exp(z - sp + cs)
    if mask is not None:
        w = jnp.where(mask, w, 0.0)
    return _dot(w, vt), carry + jnp.sum(lk, axis=-1, keepdims=True)


def _sb_kernel(bias_ref, q_ref, k_ref, v_ref, up_ref, o_ref, *, t, scale, pair):
    hp = pl.program_id(1)
    qi = pl.program_id(2)
    q2 = q_ref[0] * scale
    upper = up_ref[...]
    ri = lax.broadcasted_iota(jnp.int32, (t, t), 0)
    ci = lax.broadcasted_iota(jnp.int32, (t, t), 1)
    diag_mask = ci < ri
    q0 = pl.multiple_of(qi * t, t)
    for hh in range(pair):
        hs = slice(hh * HEAD_DIM, (hh + 1) * HEAD_DIM)
        bias = bias_ref[hp * pair + hh]
        qh = q2[:, hs].astype(BF16)
        acc, carry = _sb_tile(qh, k_ref[0, pl.ds(q0, t), hs], v_ref[0, pl.ds(q0, t), hs], bias, upper,
                              jnp.zeros((t, 1), F32), diag_mask)

        def body(it, st, hs=hs, qh=qh, bias=bias):
            acc, carry = st
            k0 = pl.multiple_of((qi - 1 - it) * t, t)
            d, carry = _sb_tile(qh, k_ref[0, pl.ds(k0, t), hs], v_ref[0, pl.ds(k0, t), hs], bias, upper,
                                carry, None)
            return acc + d, carry

        acc, _ = lax.fori_loop(0, qi, body, (acc, carry))
        o_ref[0, :, hs] = acc


def _sb_prompt(q, k, v, bias):
    B, L, W = q.shape
    pair = LANES // HEAD_DIM
    t = _row_tile(L, 128)
    upper = jnp.triu(jnp.ones((t, t), BF16), 1).T
    qspec = pl.BlockSpec((1, t, LANES), lambda b, hp, i: (b, i, hp))
    kvspec = pl.BlockSpec((1, L, LANES), lambda b, hp, i: (b, 0, hp))
    return pl.pallas_call(
        functools.partial(_sb_kernel, t=t, scale=HEAD_DIM ** -0.5, pair=pair),
        grid=(B, W // LANES, L // t),
        in_specs=[pl.BlockSpec(memory_space=pltpu.SMEM), qspec, kvspec, kvspec, _const_spec((t, t))],
        out_specs=qspec,
        out_shape=jax.ShapeDtypeStruct((B, L, W), F32),
        compiler_params=_params(("parallel", "parallel", "arbitrary")),
        name="sb_prompt",
    )(bias, q, k, v, upper)


def _sb_decode_kernel(pt_ref, q_ref, k_ref, v_ref, bias_ref, segt_ref, up_ref, hmask_ref, o_ref,
                      acc_ref, carry_ref, *, n_pages, scale):
    pg = pl.program_id(1)

    @pl.when(pg == 0)
    def _():
        acc_ref[...] = jnp.zeros_like(acc_ref)
        carry_ref[...] = jnp.zeros_like(carry_ref)

    q = q_ref[0] * scale
    prod = k_ref[0, 0] * q
    hi, lo = _split(prod)
    segt = segt_ref[...]
    nt = (((1,), (1,)), ((), ()))
    z = (lax.dot_general(segt, hi, nt, preferred_element_type=F32)
         + lax.dot_general(segt, lo, nt, preferred_element_type=F32)) + bias_ref[...]
    sp = jnp.maximum(z, 0.0) + jnp.log(1.0 + jnp.exp(-jnp.abs(z)))
    lk = -sp
    carry = carry_ref[...]
    cs = _dot_hl(lk, up_ref[...]) + carry[:, 0:1]
    w = jnp.exp(z - sp + cs)
    acc_ref[...] += _dot(w, v_ref[0, 0])
    carry_ref[...] = carry + jnp.sum(lk, axis=-1, keepdims=True)

    @pl.when(pg == n_pages - 1)
    def _():
        o_ref[0] = jnp.sum(acc_ref[...] * hmask_ref[...], axis=0, keepdims=True)


def _sb_decode(q, cache_k, cache_v, layer, page_table, bias):
    B, _, W = q.shape
    n_pages = page_table.shape[1]
    depth, n_pool, page, H, d = cache_k.shape
    ck = cache_k.reshape(depth, n_pool, page, W)
    cv = cache_v.reshape(depth, n_pool, page, W)
    head_of_lane = jnp.arange(W) // d
    segt = (head_of_lane[None, :] == jnp.arange(H)[:, None])
    upper = jnp.triu(jnp.ones((page, page), BF16), 1).T
    bias_b = jnp.broadcast_to(bias[:, None], (H, page))
    kv = pl.BlockSpec((1, 1, page, W), lambda b, p, pt: (layer, pt[b, n_pages - 1 - p], 0, 0))
    cst = lambda shape: pl.BlockSpec(shape, lambda b, p, pt: (0,) * len(shape))
    qo = pl.BlockSpec((1, 1, W), lambda b, p, pt: (b, 0, 0))
    return pl.pallas_call(
        functools.partial(_sb_decode_kernel, n_pages=n_pages, scale=d ** -0.5),
        grid_spec=pltpu.PrefetchScalarGridSpec(
            num_scalar_prefetch=1,
            grid=(B, n_pages),
            in_specs=[qo, kv, kv, cst((H, page)), cst((H, W)), cst((page, page)), cst((H, W))],
            out_specs=qo,
            scratch_shapes=[pltpu.VMEM((H, W), F32), pltpu.VMEM((H, LANES), F32)]),
        out_shape=jax.ShapeDtypeStruct((B, 1, W), F32),
        compiler_params=_params(("parallel", "arbitrary")),
        name="sb_decode",
    )(page_table, q, ck, cv, bias_b, segt.astype(BF16), upper, segt.astype(F32))


def _block_diag(m):
    G, a, b = m.shape
    eye = jnp.eye(G, dtype=m.dtype)
    return (m[:, :, None, :] * eye[:, None, :, None]).reshape(G * a, G * b)


def _layer(x, mod, P, seq_len, s5_0, rw_s0, rw_shift0, sb_fn, norm_f):
    G, R, D = x.shape
    m = lambda i, j: mod[:, :, i, j, :]
    x = _ffn(x, m(0, 0), m(0, 1), m(0, 2), P["norm_g"][0], P["w_up"][0], P["w_down"][0])
    u, p, q, k, v, gates = _inproj(x, m(1, 0), m(1, 1), P["norm_g"][1], P["w_in"], P["widths"])
    B = G * R // seq_len
    seqs = lambda t: t.reshape(B, seq_len, t.shape[-1])
    pad = (-seq_len) % SUBLANES

    def padded(t):
        t = seqs(t)
        return jnp.pad(t, ((0, 0), (0, pad), (0, 0))) if pad else t

    y_s5, s5_re, s5_im = _s5(padded(u), s5_0[0], s5_0[1], P["s5"], seq_len)
    T = 64 if seq_len % 64 == 0 else SUBLANES
    y_rw, rw_s = _rwkv(padded(p), rw_shift0, rw_s0, P["rw"], seq_len, T)
    y_sb = sb_fn(seqs(q), seqs(k), seqs(v))
    rows = lambda t: t[:, :seq_len].reshape(G, R, t.shape[-1])
    x = _merge(x, m(1, 2), rows(y_s5), rows(y_rw), rows(y_sb), gates, P["w_branch"], P["w_out"])
    x = _ffn(x, m(2, 0), m(2, 1), m(2, 2), P["norm_g"][2], P["w_up"][1], P["w_down"][1], norm_f)
    return x, (seqs(k), seqs(v), s5_re, s5_im, rw_s, seqs(p)[:, -1])


def kernel(x_prompt, x_sample, cache_k, cache_v, state_s5_re, state_s5_im, state_rwkv, state_rwkv_shift, page_table, c_prompt, c_sample, norm_g, w_ada, b_ada, w_ffn_up, w_ffn_down, w_in, s5_lam_re, s5_lam_im, s5_log_dt, s5_b_re, s5_b_im, s5_c_re, s5_c_im, s5_d, s5_w_glu, s5_b_glu, rw_mu, rw_w0, rw_w2, rw_a0, rw_a2, rw_g2, rw_k_k, rw_k_a, rw_r_k, rw_gn_w, rw_gn_b, sb_bias, w_branch, w_out, norm_f):
    bp, L, D = x_prompt.shape
    bs = x_sample.shape[0]
    depth = w_in.shape[0]
    G5, P5 = s5_lam_re.shape[1:]
    n_state = G5 * P5
    s5_w = G5 * S5_GROUP
    heads = state_rwkv.shape[2]
    W = heads * HEAD_DIM
    n_proj = state_rwkv_shift.shape[-1]
    widths = (s5_w, n_proj, W, W, W, 3 * D)

    c_all = jnp.concatenate([c_prompt, c_sample], axis=0)
    c_all = jnp.pad(c_all, ((0, (-c_all.shape[0]) % SUBLANES), (0, 0)))
    mod_all = _ada(c_all, w_ada, b_ada)
    mod_p = mod_all[:, :bp].reshape(depth, bp, 1, 3, 3, D)
    mod_s = mod_all[:, bp:bp + bs].reshape(depth, 1, bs, 3, 3, D)

    seg = (jnp.arange(W)[:, None] // HEAD_DIM == jnp.arange(W)[None, :] // HEAD_DIM).astype(BF16)
    xp = x_prompt
    xs = x_sample.reshape(1, bs, D)
    outs_p, outs_s = [], []
    for l in range(depth):
        bb_re, bb_im, pw_re, pw_im = _s5_params(s5_lam_re[l], s5_lam_im[l], s5_log_dt[l], s5_b_re[l], s5_b_im[l])
        P = dict(
            norm_g=norm_g[l], w_up=w_ffn_up[l].astype(BF16), w_down=w_ffn_down[l].astype(BF16),
            w_in=w_in[l].astype(BF16), widths=widths,
            w_branch=w_branch[l].astype(BF16), w_out=w_out[l].astype(BF16),
            s5=dict(bb_re=_block_diag(bb_re).astype(BF16), bb_im=_block_diag(bb_im).astype(BF16),
                    c_re=_block_diag(jnp.swapaxes(s5_c_re[l], 1, 2)).astype(BF16),
                    c_im=_block_diag(jnp.swapaxes(s5_c_im[l], 1, 2)).astype(BF16),
                    d=s5_d[l].reshape(1, s5_w), w_glu=s5_w_glu[l].astype(BF16), b_glu=s5_b_glu[l].reshape(1, s5_w),
                    pw_re=pw_re.reshape(S5_POWERS, n_state), pw_im=pw_im.reshape(S5_POWERS, n_state)),
            rw=dict(mu=rw_mu[l].reshape(1, n_proj), w0=rw_w0[l].reshape(1, W), w2=rw_w2[l].astype(BF16),
                    a0=rw_a0[l].reshape(1, W), a2=rw_a2[l].astype(BF16), g2=rw_g2[l].astype(BF16),
                    k_k=rw_k_k[l].reshape(1, W), k_a=rw_k_a[l].reshape(1, W), r_k=rw_r_k[l].reshape(1, W),
                    gn_w=rw_gn_w[l].reshape(1, W), gn_b=rw_gn_b[l].reshape(1, W), seg=seg))
        nf = norm_f if l == depth - 1 else None
        bias = sb_bias[l]
        zs5 = jnp.zeros((bp, 1, n_state), F32)
        xp, st = _layer(xp, mod_p[l], P, L, (zs5, zs5), jnp.zeros((bp, heads, HEAD_DIM, HEAD_DIM), F32),
                        jnp.zeros((bp, 1, n_proj), F32),
                        lambda q, k, v: _sb_prompt(q, k, v, bias), nf)
        outs_p.append(st)
        xs, st = _layer(xs, mod_s[l], P, 1,
                        (state_s5_re[l].reshape(bs, 1, n_state), state_s5_im[l].reshape(bs, 1, n_state)),
                        state_rwkv[l], state_rwkv_shift[l].reshape(bs, 1, n_proj),
                        lambda q, k, v: _sb_decode(q, cache_k, cache_v, l, page_table, bias), nf)
        outs_s.append(st)

    def stacked(outs, i, shape):
        return jnp.stack([o[i] for o in outs]).reshape((depth,) + shape)

    def group(outs, b, seq):
        return (stacked(outs, 0, (b, seq, heads, HEAD_DIM)), stacked(outs, 1, (b, seq, heads, HEAD_DIM)),
                stacked(outs, 2, (b, G5, P5)), stacked(outs, 3, (b, G5, P5)),
                stacked(outs, 4, (b, heads, HEAD_DIM, HEAD_DIM)), stacked(outs, 5, (b, n_proj)))

    return (xp, xs.reshape(bs, 1, D)) + group(outs_p, bp, L) + group(outs_s, bs, 1)
```

```python
import functools
import math

import jax
import jax.numpy as jnp
from jax import lax
from jax.experimental import pallas as pl
from jax.experimental.pallas import tpu as pltpu

F32 = jnp.float32
BF16 = jnp.bfloat16

S5_GROUP = 16
S5_STATE = 64
HEAD_DIM = 64
RW_DECAY_RANK = 64
RW_AICL_RANK = 64
RW_GATE_RANK = 128
RW_GN_EPS = 64e-5
HALF_STEP = 0.5
RMS_EPS = 1e-6
PAGE_SIZE = 128

LANES = 128
SUBLANES = 8
MXU_WIDTH = 256
VMEM_LIMIT = 52 * 1024 * 1024


def _params(sem):
    return pltpu.CompilerParams(dimension_semantics=sem, vmem_limit_bytes=VMEM_LIMIT)


def _const_spec(shape):
    nd = len(shape)
    return pl.BlockSpec(shape, lambda *_: (0,) * nd, pipeline_mode=pl.Buffered(1))


def _dot(a, b):
    return jnp.dot(a.astype(BF16), b.astype(BF16), preferred_element_type=F32)


def _dot_nt(a, b):
    return lax.dot_general(a.astype(BF16), b.astype(BF16), (((1,), (1,)), ((), ())),
                           preferred_element_type=F32)


def _dot_tn(a, b):
    return lax.dot_general(a.astype(BF16), b.astype(BF16), (((0,), (0,)), ((), ())),
                           preferred_element_type=F32)


def _split(a):
    hi = a.astype(BF16)
    lo = (a - hi.astype(F32)).astype(BF16)
    return hi, lo


def _dot_hl(a, b01):
    hi, lo = _split(a)
    return (jnp.dot(hi, b01, preferred_element_type=F32)
            + jnp.dot(lo, b01, preferred_element_type=F32))


def _dot_lh(a01, b):
    hi, lo = _split(b)
    return (jnp.dot(a01, hi, preferred_element_type=F32)
            + jnp.dot(a01, lo, preferred_element_type=F32))


def _softplus(x):
    return jnp.maximum(x, 0.0) + jnp.log1p(jnp.exp(-jnp.abs(x)))


def _norm_mod(x, g, scale, shift):
    ms = jnp.mean(x * x, axis=-1, keepdims=True)
    return (x * lax.rsqrt(ms + RMS_EPS) * g) * (1.0 + scale) + shift


def _row_tile(rows, want):
    return want if rows % want == 0 else rows


def _mod_spec(mod, tm):
    _, rm, d = mod.shape
    if rm == 1:
        return pl.BlockSpec((1, 1, d), lambda g, i: (g, 0, 0))
    return pl.BlockSpec((1, tm, d), lambda g, i: (g, i, 0))


def _ada_kernel(c_ref, w_ref, b_ref, o_ref):
    c = c_ref[...]
    o_ref[0] = _dot(c * jax.nn.sigmoid(c), w_ref[0]) + b_ref[0]


def _ada(c_all, w_ada, b_ada):
    depth, d, n = w_ada.shape
    rows = c_all.shape[0]
    tn = 1024
    return pl.pallas_call(
        _ada_kernel,
        grid=(depth, n // tn),
        in_specs=[pl.BlockSpec((rows, d), lambda l, j: (0, 0)),
                  pl.BlockSpec((1, d, tn), lambda l, j: (l, 0, j)),
                  pl.BlockSpec((1, 1, tn), lambda l, j: (l, 0, j))],
        out_specs=pl.BlockSpec((1, rows, tn), lambda l, j: (l, 0, j)),
        out_shape=jax.ShapeDtypeStruct((depth, rows, n), F32),
        compiler_params=_params(("parallel", "parallel")),
        name="adaln",
    )(c_all, w_ada, b_ada.reshape(depth, 1, n))


def _ffn_kernel(x_ref, sh_ref, sc_ref, gt_ref, g_ref, wup_ref, wdn_ref, *rest, dff, chunk, final):
    if final:
        nf_ref, o_ref, acc_ref = rest
    else:
        o_ref, acc_ref = rest
    x = x_ref[0]
    h = _norm_mod(x, g_ref[...], sc_ref[0], sh_ref[0]).astype(BF16)
    for j in range(dff // chunk):
        c0, c1 = j * chunk, (j + 1) * chunk
        g = jnp.dot(h, wup_ref[:, c0:c1], preferred_element_type=F32)
        u = jnp.dot(h, wup_ref[:, dff + c0:dff + c1], preferred_element_type=F32)
        act = (g * jax.nn.sigmoid(g) * u).astype(BF16)
        d = jnp.dot(act, wdn_ref[c0:c1, :], preferred_element_type=F32)
        if j == 0:
            acc_ref[...] = d
        else:
            acc_ref[...] += d
    y = x + HALF_STEP * gt_ref[0] * acc_ref[...]
    if final:
        ms = jnp.mean(y * y, axis=-1, keepdims=True)
        y = y * lax.rsqrt(ms + RMS_EPS) * nf_ref[...]
    o_ref[0] = y


def _ffn(x, shift, scale, gate, g, w_up, w_down, norm_f=None):
    G, R, D = x.shape
    dff = w_down.shape[0]
    tm = _row_tile(R, 512)
    final = norm_f is not None
    in_specs = [pl.BlockSpec((1, tm, D), lambda gi, i: (gi, i, 0)),
                _mod_spec(shift, tm), _mod_spec(scale, tm), _mod_spec(gate, tm),
                _const_spec((1, D)), _const_spec(w_up.shape), _const_spec(w_down.shape)]
    args = [x, shift, scale, gate, g.reshape(1, D), w_up, w_down]
    if final:
        in_specs.append(_const_spec((1, D)))
        args.append(norm_f.reshape(1, D))
    return pl.pallas_call(
        functools.partial(_ffn_kernel, dff=dff, chunk=MXU_WIDTH, final=final),
        grid=(G, R // tm),
        in_specs=in_specs,
        out_specs=pl.BlockSpec((1, tm, D), lambda gi, i: (gi, i, 0)),
        out_shape=jax.ShapeDtypeStruct((G, R, D), F32),
        scratch_shapes=[pltpu.VMEM((tm, D), F32)],
        compiler_params=_params(("parallel", "parallel")),
        name="ffn",
    )(*args)


def _inproj_kernel(x_ref, sh_ref, sc_ref, g_ref, w_ref, *o_refs, widths, chunk):
    h = _norm_mod(x_ref[0], g_ref[...], sc_ref[0], sh_ref[0]).astype(BF16)
    off = 0
    for o_ref, wd in zip(o_refs, widths):
        for c0 in range(0, wd, chunk):
            o_ref[0, :, c0:c0 + chunk] = jnp.dot(h, w_ref[:, off + c0:off + c0 + chunk],
                                                 preferred_element_type=F32)
        off += wd


def _inproj(x, shift, scale, g, w_in, widths):
    G, R, D = x.shape
    tm = _row_tile(R, 256)
    return pl.pallas_call(
        functools.partial(_inproj_kernel, widths=widths, chunk=MXU_WIDTH),
        grid=(G, R // tm),
        in_specs=[pl.BlockSpec((1, tm, D), lambda gi, i: (gi, i, 0)),
                  _mod_spec(shift, tm), _mod_spec(scale, tm),
                  _const_spec((1, D)), _const_spec(w_in.shape)],
        out_specs=[pl.BlockSpec((1, tm, wd), lambda gi, i: (gi, i, 0)) for wd in widths],
        out_shape=[jax.ShapeDtypeStruct((G, R, wd), F32) for wd in widths],
        compiler_params=_params(("parallel", "parallel")),
        name="inproj",
    )(x, shift, scale, g.reshape(1, D), w_in)


def _merge_kernel(x_ref, gt_ref, y0_ref, y1_ref, y2_ref, gates_ref, wb_ref, wo_ref, o_ref, *, d, bw):
    merged = None
    for i, y_ref in enumerate((y0_ref, y1_ref, y2_ref)):
        t = jax.nn.sigmoid(gates_ref[0, :, i * d:(i + 1) * d]) * _dot(y_ref[0], wb_ref[i * bw:(i + 1) * bw, :])
        merged = t if merged is None else merged + t
    o_ref[0] = x_ref[0] + gt_ref[0] * _dot(merged, wo_ref[...])


def _merge(x, gate, y_s5, y_rw, y_sb, gates, w_branch, w_out):
    G, R, D = x.shape
    bw = y_s5.shape[-1]
    tm = _row_tile(R, 512)
    row = lambda wd: pl.BlockSpec((1, tm, wd), lambda gi, i: (gi, i, 0))
    return pl.pallas_call(
        functools.partial(_merge_kernel, d=D, bw=bw),
        grid=(G, R // tm),
        in_specs=[row(D), _mod_spec(gate, tm), row(bw), row(bw), row(bw), row(3 * D),
                  _const_spec(w_branch.shape), _const_spec(w_out.shape)],
        out_specs=row(D),
        out_shape=jax.ShapeDtypeStruct((G, R, D), F32),
        compiler_params=_params(("parallel", "parallel")),
        name="merge",
    )(x, gate, y_s5, y_rw, y_sb, gates, w_branch, w_out)


S5_POWERS = SUBLANES


def _s5_param_kernel(lr_ref, li_ref, ldt_ref, br_ref, bi_ref, bbr_ref, bbi_ref, pwr_ref, pwi_ref):
    lr, li = lr_ref[...], li_ref[...]
    dt = jnp.exp(ldt_ref[...])
    mag = jnp.exp(lr * dt)
    ab_re, ab_im = mag * jnp.cos(li * dt), mag * jnp.sin(li * dt)
    den = lr * lr + li * li
    nr, ni = ab_re - 1.0, ab_im
    f_re = (nr * lr + ni * li) / den
    f_im = (ni * lr - nr * li) / den
    br, bi = br_ref[...], bi_ref[...]
    bbr_ref[...] = f_re * br - f_im * bi
    bbi_ref[...] = f_re * bi + f_im * br
    pr, pi = ab_re, ab_im
    for j in range(S5_POWERS):
        pwr_ref[j] = pr
        pwi_ref[j] = pi
        pr, pi = pr * ab_re - pi * ab_im, pr * ab_im + pi * ab_re


def _s5_params(lam_re, lam_im, log_dt, b_re, b_im):
    G, P = lam_re.shape
    C = b_re.shape[-1]
    sds = jax.ShapeDtypeStruct
    return pl.pallas_call(
        _s5_param_kernel,
        out_shape=[sds((G, C, P), F32), sds((G, C, P), F32),
                   sds((S5_POWERS, G, 1, P), F32), sds((S5_POWERS, G, 1, P), F32)],
        name="s5_params",
    )(lam_re.reshape(G, 1, P), lam_im.reshape(G, 1, P),
      jnp.broadcast_to(log_dt[:, None, None], (G, 1, P)),
      jnp.swapaxes(b_re, 1, 2), jnp.swapaxes(b_im, 1, 2))


S5_LANE_CHUNK = 256


def _s5_kernel(u_ref, x0r_ref, x0i_ref, bbr_ref, bbi_ref, cr_ref, ci_ref, d_ref, wg_ref, bg_ref,
               pwr_ref, pwi_ref, y_ref, xr_out, xi_out, sr_ref, si_ref, car_ref, cai_ref,
               *, tl, nl, last_row):
    li = pl.program_id(1)

    @pl.when(li == 0)
    def _():
        car_ref[...] = x0r_ref[0]
        cai_ref[...] = x0i_ref[0]

    u = u_ref[0]
    ub = u.astype(BF16)
    sr_ref[...] = jnp.dot(ub, bbr_ref[...], preferred_element_type=F32)
    si_ref[...] = jnp.dot(ub, bbi_ref[...], preferred_element_type=F32)
    n_state = sr_ref.shape[1]
    lc = S5_LANE_CHUNK
    row = lax.broadcasted_iota(jnp.int32, (SUBLANES, lc), 0)
    for c in range(n_state // lc):
        ls = slice(c * lc, (c + 1) * lc)
        pr8, pi8 = pwr_ref[:, ls], pwi_ref[:, ls]
        steps = []
        for dd in (1, 2, 4):
            steps.append((dd, jnp.where(row >= dd, pr8[dd - 1:dd, :], 0.0),
                          jnp.where(row >= dd, pi8[dd - 1:dd, :], 0.0)))

        def body(i, carry, ls=ls, pr8=pr8, pi8=pi8, steps=steps):
            car, cai = carry
            r0 = pl.multiple_of(i * SUBLANES, SUBLANES)
            xr = sr_ref[pl.ds(r0, SUBLANES), ls]
            xi = si_ref[pl.ds(r0, SUBLANES), ls]
            for dd, ar, ai in steps:
                rr = pltpu.roll(xr, dd, 0)
                ri = pltpu.roll(xi, dd, 0)
                xr, xi = xr + ar * rr - ai * ri, xi + ar * ri + ai * rr
            xr, xi = xr + pr8 * car - pi8 * cai, xi + pr8 * cai + pi8 * car
            sr_ref[pl.ds(r0, SUBLANES), ls] = xr
            si_ref[pl.ds(r0, SUBLANES), ls] = xi
            return xr[SUBLANES - 1:SUBLANES, :], xi[SUBLANES - 1:SUBLANES, :]

        car, cai = lax.fori_loop(0, tl // SUBLANES, body, (car_ref[:, ls], cai_ref[:, ls]))
        car_ref[:, ls] = car
        cai_ref[:, ls] = cai

    y = (_dot(sr_ref[...], cr_ref[...]) - _dot(si_ref[...], ci_ref[...]) + d_ref[...] * u)
    z = jax.nn.gelu(y)
    y_ref[0] = z * jax.nn.sigmoid(_dot(z, wg_ref[...]) + bg_ref[...])

    @pl.when(li == nl - 1)
    def _():
        xr_out[0] = sr_ref[last_row:last_row + 1, :]
        xi_out[0] = si_ref[last_row:last_row + 1, :]


def _s5(u, x0_re, x0_im, prm, seq_len):
    B, Lp, W = u.shape
    n_state = prm["bb_re"].shape[1]
    tl = _row_tile(Lp, 256)
    nl = Lp // tl
    last_row = (seq_len - 1) - (nl - 1) * tl
    assert 0 <= last_row < tl
    state = pl.BlockSpec((1, 1, n_state), lambda b, i: (b, 0, 0))
    sds = jax.ShapeDtypeStruct
    return pl.pallas_call(
        functools.partial(_s5_kernel, tl=tl, nl=nl, last_row=last_row),
        grid=(B, nl),
        in_specs=[pl.BlockSpec((1, tl, W), lambda b, i: (b, i, 0)), state, state,
                  _const_spec(prm["bb_re"].shape), _const_spec(prm["bb_im"].shape),
                  _const_spec(prm["c_re"].shape), _const_spec(prm["c_im"].shape),
                  _const_spec((1, W)), _const_spec((W, W)), _const_spec((1, W)),
                  _const_spec((S5_POWERS, n_state)), _const_spec((S5_POWERS, n_state))],
        out_specs=[pl.BlockSpec((1, tl, W), lambda b, i: (b, i, 0)), state, state],
        out_shape=[sds((B, Lp, W), F32), sds((B, 1, n_state), F32), sds((B, 1, n_state), F32)],
        scratch_shapes=[pltpu.VMEM((tl, n_state), F32), pltpu.VMEM((tl, n_state), F32),
                        pltpu.VMEM((1, n_state), F32), pltpu.VMEM((1, n_state), F32)],
        compiler_params=_params(("parallel", "arbitrary")),
        name="s5",
    )(u, x0_re, x0_im, prm["bb_re"], prm["bb_im"], prm["c_re"], prm["c_im"], prm["d"],
      prm["w_glu"], prm["b_glu"], prm["pw_re"], prm["pw_im"])


def _rwkv_kernel(p_ref, sh0_ref, s0_ref, mu_ref, w0_ref, w2_ref, a0_ref, a2_ref, g2_ref, kk_ref, ka_ref,
                 rk_ref, gnw_ref, gnb_ref, seg_ref, tri_ref, y_ref, sfin_ref, S_ref, prev_ref,
                 *, T, nl, valid, heads):
    li = pl.program_id(1)
    W = heads * HEAD_DIM

    @pl.when(li == 0)
    def _():
        S_ref[...] = s0_ref[0]
        prev_ref[...] = sh0_ref[0]

    p = p_ref[0]
    trow = lax.broadcasted_iota(jnp.int32, (T, 1), 0)
    p_prev = jnp.where(trow == 0, prev_ref[...], pltpu.roll(p, 1, 0))
    prev_ref[...] = p[T - 1:T, :]
    ps = p + (p_prev - p) * mu_ref[...]
    r, k, v = ps[:, 0:W], ps[:, W:2 * W], ps[:, 2 * W:3 * W]
    o3 = 3 * W
    dw = ps[:, o3:o3 + RW_DECAY_RANK]
    da = ps[:, o3 + RW_DECAY_RANK:o3 + RW_DECAY_RANK + RW_AICL_RANK]
    dg = ps[:, o3 + RW_DECAY_RANK + RW_AICL_RANK:]
    w_log = -_softplus(-(w0_ref[...] + _dot(jnp.tanh(dw), w2_ref[...]))) - 0.5
    lw = -jnp.exp(w_log)
    a = jax.nn.sigmoid(a0_ref[...] + _dot(da, a2_ref[...]))
    g = _dot(jax.nn.sigmoid(dg), g2_ref[...])
    seg = seg_ref[...]
    kk = k * kk_ref[...]
    kk = kk / jnp.maximum(jnp.sqrt(_dot_hl(kk * kk, seg)), 1e-12)
    km = k * (1.0 + (a - 1.0) * ka_ref[...])
    bonus = _dot_hl(r * km * rk_ref[...], seg)
    if valid < T:
        live = trow < valid
        lw = jnp.where(live, lw, 0.0)
        kk = jnp.where(live, kk, 0.0)
        km = jnp.where(live, km, 0.0)

    cum = _dot_lh(tri_ref[...], lw)
    c_end = cum[T - 1:T, :]
    e_neg = jnp.exp(-cum)
    e_end = jnp.exp(c_end - cum)
    kka = kk * a
    A = -kk * jnp.exp(cum - lw)
    Bt = kka * e_neg
    Kt = km * e_neg
    Rt = r * jnp.exp(cum)
    Bh = kka * e_end
    Kh = km * e_end
    w_end = jnp.exp(c_end)

    ti = lax.broadcasted_iota(jnp.int32, (T, T), 0)
    tj = lax.broadcasted_iota(jnp.int32, (T, T), 1)
    strict = ti > tj
    incl = ti >= tj
    eye = jnp.where(ti == tj, 1.0, 0.0)
    H = range(heads)
    hsl = [slice(h * HEAD_DIM, (h + 1) * HEAD_DIM) for h in H]
    Ss = [S_ref[h] for h in H]
    ARs = [jnp.concatenate([A[:, hs], Rt[:, hs]], axis=0) for hs in hsl]
    BKs = [jnp.concatenate([Bt[:, hs], Kt[:, hs]], axis=0) for hs in hsl]
    Gms = [_dot_nt(ARs[h], BKs[h]) for h in H]
    ARSs = [_dot_nt(ARs[h], Ss[h]) for h in H]
    Ns = [jnp.where(strict, Gm[0:T, 0:T], 0.0) for Gm in Gms]
    Ms = [jnp.where(strict, Gm[0:T, T:2 * T], 0.0) for Gm in Gms]
    Prbs = [jnp.where(incl, Gm[T:2 * T, 0:T], 0.0) for Gm in Gms]
    Prks = [jnp.where(incl, Gm[T:2 * T, T:2 * T], 0.0) for Gm in Gms]
    vhs = [v[:, hs] for hs in hsl]
    MVs = [_dot(Ms[h], vhs[h]) for h in H]
    Xs = [eye + N for N in Ns]
    Ps = Ns
    n = 1
    while 2 * n < T:
        Ps = [_dot(P, P) for P in Ps]
        Xs = [Xs[h] + _dot(Xs[h], Ps[h]) for h in H]
        n *= 2
    Us = [_dot(Xs[h], ARSs[h][0:T] + MVs[h]) for h in H]
    Os = [ARSs[h][T:2 * T] + _dot(Prbs[h], Us[h]) + _dot(Prks[h], vhs[h]) for h in H]
    for h in H:
        hs = hsl[h]
        S_ref[h] = Ss[h] * w_end[:, hs] + _dot_tn(Us[h], Bh[:, hs]) + _dot_tn(vhs[h], Kh[:, hs])
    for h in H:
        hs = hsl[h]
        O = Os[h]
        mean = jnp.mean(O, axis=-1, keepdims=True)
        var = jnp.mean(jnp.square(O - mean), axis=-1, keepdims=True)
        on = (O - mean) * lax.rsqrt(var + RW_GN_EPS) * gnw_ref[:, hs] + gnb_ref[:, hs]
        on = on + bonus[:, hs] * vhs[h]
        y_ref[0, :, hs] = on * g[:, hs]

    @pl.when(li == nl - 1)
    def _():
        sfin_ref[0] = S_ref[...]


def _rwkv(p, shift0, s0, prm, seq_len, T):
    B, Lp, NP = p.shape
    heads = s0.shape[1]
    W = heads * HEAD_DIM
    nl = Lp // T
    valid = seq_len - (nl - 1) * T
    assert nl * T == Lp and 0 < valid <= T and (nl == 1 or valid == T)
    tri = jnp.tril(jnp.ones((T, T), BF16))
    vec = lambda n: _const_spec((1, n))
    sds = jax.ShapeDtypeStruct
    st_spec = pl.BlockSpec((1, heads, HEAD_DIM, HEAD_DIM), lambda b, i: (b, 0, 0, 0))
    return pl.pallas_call(
        functools.partial(_rwkv_kernel, T=T, nl=nl, valid=valid, heads=heads),
        grid=(B, nl),
        in_specs=[pl.BlockSpec((1, T, NP), lambda b, i: (b, i, 0)),
                  pl.BlockSpec((1, 1, NP), lambda b, i: (b, 0, 0)), st_spec,
                  vec(NP), vec(W), _const_spec((RW_DECAY_RANK, W)), vec(W), _const_spec((RW_AICL_RANK, W)),
                  _const_spec((RW_GATE_RANK, W)), vec(W), vec(W), vec(W), vec(W), vec(W),
                  _const_spec((W, W)), _const_spec((T, T))],
        out_specs=[pl.BlockSpec((1, T, W), lambda b, i: (b, i, 0)), st_spec],
        out_shape=[sds((B, Lp, W), F32), sds(s0.shape, F32)],
        scratch_shapes=[pltpu.VMEM((heads, HEAD_DIM, HEAD_DIM), F32), pltpu.VMEM((1, NP), F32)],
        compiler_params=_params(("parallel", "arbitrary")),
        name="rwkv",
    )(p, shift0, s0, prm["mu"], prm["w0"], prm["w2"], prm["a0"], prm["a2"], prm["g2"], prm["k_k"],
      prm["k_a"], prm["r_k"], prm["gn_w"], prm["gn_b"], prm["seg"], tri)


SB_TILE = MXU_WIDTH
SB_Q_SUB = 2
SB_CAST_ROWS = 512
SB_BIAS_SPLIT = 3
LOG2E = math.log2(math.e)


def _neg_abs(x):
    bits = lax.bitcast_convert_type(x, jnp.uint32) | jnp.uint32(0x80000000)
    return lax.bitcast_convert_type(bits, F32)


def _sb_tiles(chains, upper):
    n = range(len(chains))
    masks = [c[4] for c in chains]
    nzs = [_dot_nt(c[0], c[1]) for c in chains]
    ls = [jnp.log2(1.0 + jnp.exp2(_neg_abs(nz))) for nz in nzs]
    lks = [jnp.minimum(nzs[i], 0.0) - ls[i] for i in n]
    lks = [lks[i] if masks[i] is None else jnp.where(masks[i], lks[i], 0.0) for i in n]
    hls = [jnp.concatenate(_split(lk), axis=1) for lk in lks]
    css = [jnp.dot(hls[i], upper, preferred_element_type=F32) + chains[i][3] for i in n]
    ws = [jnp.exp2(lks[i] - nzs[i] + css[i]) for i in n]
    ws = [ws[i] if masks[i] is None else jnp.where(masks[i], ws[i], 0.0) for i in n]
    ds = [_dot(ws[i], chains[i][2]) for i in n]
    return ds, [chains[i][3] + jnp.sum(lks[i], axis=-1, keepdims=True) for i in n]


def _sb_kernel(q_ref, k_ref, v_ref, bias_ref, up_ref, o_ref, kb_ref, vb_ref, *, t, nsub, scale, pair, L):
    qi = pl.program_id(2)
    heads = range(pair)
    lane = lax.broadcasted_iota(jnp.int32, (1, LANES), 1)
    is_data = lane < HEAD_DIM

    def head_lanes(x, hh):
        return x if hh == 0 else pltpu.roll(x, LANES - hh * HEAD_DIM, 1)

    @pl.when(qi == 0)
    def _():
        rows = min(SB_CAST_ROWS, L)
        ones = jnp.where(lane < HEAD_DIM + SB_BIAS_SPLIT, 1.0, 0.0)

        def cast(c, _):
            r0 = pl.multiple_of(c * rows, rows)
            kk = k_ref[0, pl.ds(r0, rows), :]
            vv = v_ref[0, pl.ds(r0, rows), :]
            for hh in heads:
                kb_ref[hh, pl.ds(r0, rows), :] = jnp.where(is_data, head_lanes(kk, hh), ones).astype(BF16)
                vb_ref[hh, pl.ds(r0, rows), :] = head_lanes(vv, hh)[:, 0:HEAD_DIM].astype(BF16)
            return 0

        lax.fori_loop(0, L // rows, cast, 0)

    q2 = q_ref[0] * (-scale * LOG2E)
    qs = [[jnp.where(is_data, head_lanes(q2[s * t:(s + 1) * t, :], hh), bias_ref[0, hh:hh + 1, :]).astype(BF16)
           for hh in heads] for s in range(nsub)]
    upper = up_ref[...]
    ri = lax.broadcasted_iota(jnp.int32, (t, t), 0)
    ci = lax.broadcasted_iota(jnp.int32, (t, t), 1)
    diag_mask = ci < ri

    def tiles(k0, subs, carries, masks):
        chains = [(qs[s][hh], kb_ref[hh, pl.ds(k0, t), :], vb_ref[hh, pl.ds(k0, t), :],
                   carries[i * pair + hh], masks[i]) for i, s in enumerate(subs) for hh in heads]
        return _sb_tiles(chains, upper)

    zero = jnp.zeros((t, 1), F32)
    accs = [None] * (nsub * pair)
    carries = [zero] * (nsub * pair)
    for kt in reversed(range(nsub)):
        subs = list(range(kt, nsub))
        k0 = pl.multiple_of((qi * nsub + kt) * t, t)
        ds, cs = tiles(k0, subs, [carries[s * pair + hh] for s in subs for hh in heads],
                       [diag_mask if s == kt else None for s in subs])
        for i, s in enumerate(subs):
            for hh in heads:
                c = s * pair + hh
                accs[c] = ds[i * pair + hh] if accs[c] is None else accs[c] + ds[i * pair + hh]
                carries[c] = cs[i * pair + hh]

    def body(it, st):
        accs, carries = st
        k0 = pl.multiple_of((qi * nsub - 1 - it) * t, t)
        ds, carries = tiles(k0, list(range(nsub)), list(carries), [None] * nsub)
        return tuple(a + d for a, d in zip(accs, ds)), tuple(carries)

    accs, _ = lax.fori_loop(0, qi * nsub, body, (tuple(accs), tuple(carries)))
    for s in range(nsub):
        for hh in heads:
            o_ref[0, s * t:(s + 1) * t, hh * HEAD_DIM:(hh + 1) * HEAD_DIM] = accs[s * pair + hh]


def _sb_prompt(q, k, v, bias):
    B, L, W = q.shape
    H = W // HEAD_DIM
    pair = LANES // HEAD_DIM
    t = _row_tile(L, SB_TILE)
    nsub = SB_Q_SUB if L % (SB_Q_SUB * t) == 0 else 1
    upper = jnp.triu(jnp.ones((t, t), BF16), 1).T
    upper = jnp.concatenate([upper, upper], axis=0)
    rest = -bias * LOG2E
    parts = []
    for _ in range(SB_BIAS_SPLIT):
        parts.append(rest.astype(BF16).astype(F32))
        rest = rest - parts[-1]
    bias_rows = jnp.zeros((H, LANES), F32).at[:, HEAD_DIM:HEAD_DIM + SB_BIAS_SPLIT].set(jnp.stack(parts, axis=1))
    qspec = pl.BlockSpec((1, nsub * t, LANES), lambda b, hp, i: (b, i, hp))
    kvspec = pl.BlockSpec((1, L, LANES), lambda b, hp, i: (b, 0, hp))
    return pl.pallas_call(
        functools.partial(_sb_kernel, t=t, nsub=nsub, scale=HEAD_DIM ** -0.5, pair=pair, L=L),
        grid=(B, W // LANES, L // (nsub * t)),
        in_specs=[qspec, kvspec, kvspec, pl.BlockSpec((1, pair, LANES), lambda b, hp, i: (hp, 0, 0)),
                  _const_spec((2 * t, t))],
        out_specs=qspec,
        out_shape=jax.ShapeDtypeStruct((B, L, W), F32),
        scratch_shapes=[pltpu.VMEM((pair, L, LANES), BF16), pltpu.VMEM((pair, L, HEAD_DIM), BF16)],
        compiler_params=_params(("parallel", "parallel", "arbitrary")),
        name="sb_prompt",
    )(q, k, v, bias_rows.reshape(H // pair, pair, LANES), upper)


SB_DEC_PAGES = 8


def _sb_decode_kernel(pt_ref, q_ref, nbias_ref, *refs, pps, page, scale):
    k_refs, v_refs = refs[:pps], refs[pps:2 * pps]
    o_ref, acc_ref, carry_ref = refs[2 * pps:]
    step = pl.program_id(1)
    H, d = q_ref.shape[1:]
    n = page * H

    @pl.when(step == 0)
    def _():
        acc_ref[...] = jnp.zeros_like(acc_ref)
        carry_ref[...] = jnp.zeros_like(carry_ref)

    qb = (q_ref[0] * (-scale * LOG2E)).astype(BF16)
    lane = lax.broadcasted_iota(jnp.int32, (H, n), 1)
    sel = (lane % H) == lax.broadcasted_iota(jnp.int32, (H, n), 0)
    lane1 = lane[0:1, :]
    pages = range(pps)
    k2s = [k_refs[i][0, 0].reshape(n, d).astype(BF16) for i in pages]
    v2s = [v_refs[i][0, 0].reshape(n, d).astype(BF16) for i in pages]
    gs = [_dot_nt(qb, k2s[i]) for i in pages]
    nzs = [jnp.sum(jnp.where(sel, g, 0.0), axis=0, keepdims=True) + nbias_ref[...] for g in gs]
    lks = [jnp.minimum(nz, 0.0) - jnp.log2(1.0 + jnp.exp2(_neg_abs(nz))) for nz in nzs]
    sufs, tots = list(lks), list(lks)
    sh = H
    while sh < n:
        sufs = [x + jnp.where(lane1 < n - sh, pltpu.roll(x, n - sh, 1), 0.0) for x in sufs]
        tots = [x + pltpu.roll(x, sh, 1) for x in tots]
        sh *= 2
    carry = carry_ref[...]
    css = []
    for i in pages:
        css.append(sufs[i] - lks[i] + carry)
        carry = carry + tots[i]
    carry_ref[...] = carry
    ws = [jnp.exp2(lks[i] - nzs[i] + css[i]) for i in pages]
    wms = [jnp.where(sel, jnp.broadcast_to(w, (H, n)), 0.0) for w in ws]
    ds = [_dot(wms[i], v2s[i]) for i in pages]
    acc = acc_ref[...]
    for dd in ds:
        acc = acc + dd
    acc_ref[...] = acc

    @pl.when(step == pl.num_programs(1) - 1)
    def _():
        o_ref[0] = acc


def _sb_decode(q, cache_k, cache_v, layer, page_table, bias):
    B, _, W = q.shape
    n_pages = page_table.shape[1]
    depth, n_pool, page, H, d = cache_k.shape
    pps = SB_DEC_PAGES if n_pages % SB_DEC_PAGES == 0 else 1
    nbias = jnp.tile(-bias * LOG2E, page).reshape(1, page * H)
    kv = [pl.BlockSpec((1, 1, page, H, d),
                       lambda b, s, pt, i=i: (layer, pt[b, n_pages - 1 - (s * pps + i)], 0, 0, 0))
          for i in range(pps)]
    qo = pl.BlockSpec((1, H, d), lambda b, s, pt: (b, 0, 0))
    out = pl.pallas_call(
        functools.partial(_sb_decode_kernel, pps=pps, page=page, scale=d ** -0.5),
        grid_spec=pltpu.PrefetchScalarGridSpec(
            num_scalar_prefetch=1,
            grid=(B, n_pages // pps),
            in_specs=[qo, pl.BlockSpec((1, page * H), lambda b, s, pt: (0, 0))] + kv + kv,
            out_specs=qo,
            scratch_shapes=[pltpu.VMEM((H, d), F32), pltpu.VMEM((1, page * H), F32)]),
        out_shape=jax.ShapeDtypeStruct((B, H, d), F32),
        compiler_params=_params(("parallel", "arbitrary")),
        name="sb_decode",
    )(page_table, q.reshape(B, H, d), nbias, *([cache_k] * pps), *([cache_v] * pps))
    return out.reshape(B, 1, W)


def _block_diag(m):
    G, a, b = m.shape
    eye = jnp.eye(G, dtype=m.dtype)
    return (m[:, :, None, :] * eye[:, None, :, None]).reshape(G * a, G * b)


def _layer(x, mod, P, seq_len, s5_0, rw_s0, rw_shift0, sb_fn, norm_f):
    G, R, D = x.shape
    m = lambda i, j: mod[:, :, i, j, :]
    x = _ffn(x, m(0, 0), m(0, 1), m(0, 2), P["norm_g"][0], P["w_up"][0], P["w_down"][0])
    u, p, q, k, v, gates = _inproj(x, m(1, 0), m(1, 1), P["norm_g"][1], P["w_in"], P["widths"])
    B = G * R // seq_len
    seqs = lambda t: t.reshape(B, seq_len, t.shape[-1])
    pad = (-seq_len) % SUBLANES

    def padded(t):
        t = seqs(t)
        return jnp.pad(t, ((0, 0), (0, pad), (0, 0))) if pad else t

    y_s5, s5_re, s5_im = _s5(padded(u), s5_0[0], s5_0[1], P["s5"], seq_len)
    T = 64 if seq_len % 64 == 0 else SUBLANES
    y_rw, rw_s = _rwkv(padded(p), rw_shift0, rw_s0, P["rw"], seq_len, T)
    y_sb = sb_fn(seqs(q), seqs(k), seqs(v))
    rows = lambda t: t[:, :seq_len].reshape(G, R, t.shape[-1])
    x = _merge(x, m(1, 2), rows(y_s5), rows(y_rw), rows(y_sb), gates, P["w_branch"], P["w_out"])
    x = _ffn(x, m(2, 0), m(2, 1), m(2, 2), P["norm_g"][2], P["w_up"][1], P["w_down"][1], norm_f)
    return x, (seqs(k), seqs(v), s5_re, s5_im, rw_s, seqs(p)[:, -1])


def kernel(x_prompt, x_sample, cache_k, cache_v, state_s5_re, state_s5_im, state_rwkv, state_rwkv_shift, page_table, c_prompt, c_sample, norm_g, w_ada, b_ada, w_ffn_up, w_ffn_down, w_in, s5_lam_re, s5_lam_im, s5_log_dt, s5_b_re, s5_b_im, s5_c_re, s5_c_im, s5_d, s5_w_glu, s5_b_glu, rw_mu, rw_w0, rw_w2, rw_a0, rw_a2, rw_g2, rw_k_k, rw_k_a, rw_r_k, rw_gn_w, rw_gn_b, sb_bias, w_branch, w_out, norm_f):
    bp, L, D = x_prompt.shape
    bs = x_sample.shape[0]
    depth = w_in.shape[0]
    G5, P5 = s5_lam_re.shape[1:]
    n_state = G5 * P5
    s5_w = G5 * S5_GROUP
    heads = state_rwkv.shape[2]
    W = heads * HEAD_DIM
    n_proj = state_rwkv_shift.shape[-1]
    widths = (s5_w, n_proj, W, W, W, 3 * D)

    c_all = jnp.concatenate([c_prompt, c_sample], axis=0)
    c_all = jnp.pad(c_all, ((0, (-c_all.shape[0]) % SUBLANES), (0, 0)))
    mod_all = _ada(c_all, w_ada, b_ada)
    mod_p = mod_all[:, :bp].reshape(depth, bp, 1, 3, 3, D)
    mod_s = mod_all[:, bp:bp + bs].reshape(depth, 1, bs, 3, 3, D)

    seg = (jnp.arange(W)[:, None] // HEAD_DIM == jnp.arange(W)[None, :] // HEAD_DIM).astype(BF16)
    xp = x_prompt
    xs = x_sample.reshape(1, bs, D)
    outs_p, outs_s = [], []
    for l in range(depth):
        bb_re, bb_im, pw_re, pw_im = _s5_params(s5_lam_re[l], s5_lam_im[l], s5_log_dt[l], s5_b_re[l], s5_b_im[l])
        P = dict(
            norm_g=norm_g[l], w_up=w_ffn_up[l].astype(BF16), w_down=w_ffn_down[l].astype(BF16),
            w_in=w_in[l].astype(BF16), widths=widths,
            w_branch=w_branch[l].astype(BF16), w_out=w_out[l].astype(BF16),
            s5=dict(bb_re=_block_diag(bb_re).astype(BF16), bb_im=_block_diag(bb_im).astype(BF16),
                    c_re=_block_diag(jnp.swapaxes(s5_c_re[l], 1, 2)).astype(BF16),
                    c_im=_block_diag(jnp.swapaxes(s5_c_im[l], 1, 2)).astype(BF16),
                    d=s5_d[l].reshape(1, s5_w), w_glu=s5_w_glu[l].astype(BF16), b_glu=s5_b_glu[l].reshape(1, s5_w),
                    pw_re=pw_re.reshape(S5_POWERS, n_state), pw_im=pw_im.reshape(S5_POWERS, n_state)),
            rw=dict(mu=rw_mu[l].reshape(1, n_proj), w0=rw_w0[l].reshape(1, W), w2=rw_w2[l].astype(BF16),
                    a0=rw_a0[l].reshape(1, W), a2=rw_a2[l].astype(BF16), g2=rw_g2[l].astype(BF16),
                    k_k=rw_k_k[l].reshape(1, W), k_a=rw_k_a[l].reshape(1, W), r_k=rw_r_k[l].reshape(1, W),
                    gn_w=rw_gn_w[l].reshape(1, W), gn_b=rw_gn_b[l].reshape(1, W), seg=seg))
        nf = norm_f if l == depth - 1 else None
        bias = sb_bias[l]
        zs5 = jnp.zeros((bp, 1, n_state), F32)
        xp, st = _layer(xp, mod_p[l], P, L, (zs5, zs5), jnp.zeros((bp, heads, HEAD_DIM, HEAD_DIM), F32),
                        jnp.zeros((bp, 1, n_proj), F32),
                        lambda q, k, v: _sb_prompt(q, k, v, bias), nf)
        outs_p.append(st)
        xs, st = _layer(xs, mod_s[l], P, 1,
                        (state_s5_re[l].reshape(bs, 1, n_state), state_s5_im[l].reshape(bs, 1, n_state)),
                        state_rwkv[l], state_rwkv_shift[l].reshape(bs, 1, n_proj),
                        lambda q, k, v: _sb_decode(q, cache_k, cache_v, l, page_table, bias), nf)
        outs_s.append(st)

    def stacked(outs, i, shape):
        return jnp.stack([o[i] for o in outs]).reshape((depth,) + shape)

    def group(outs, b, seq):
        return (stacked(outs, 0, (b, seq, heads, HEAD_DIM)), stacked(outs, 1, (b, seq, heads, HEAD_DIM)),
                stacked(outs, 2, (b, G5, P5)), stacked(outs, 3, (b, G5, P5)),
                stacked(outs, 4, (b, heads, HEAD_DIM, HEAD_DIM)), stacked(outs, 5, (b, n_proj)))

    return (xp, xs.reshape(bs, 1, D)) + group(outs_p, bp, L) + group(outs_s, bs, 1)
```

```python
import functools
import math

import jax
import jax.numpy as jnp
from jax import lax
from jax.experimental import pallas as pl
from jax.experimental.pallas import tpu as pltpu

F32 = jnp.float32
BF16 = jnp.bfloat16

S5_GROUP = 16
S5_STATE = 64
HEAD_DIM = 64
RW_DECAY_RANK = 64
RW_AICL_RANK = 64
RW_GATE_RANK = 128
RW_GN_EPS = 64e-5
HALF_STEP = 0.5
RMS_EPS = 1e-6
PAGE_SIZE = 128

LANES = 128
SUBLANES = 8
MXU_WIDTH = 256
VMEM_LIMIT = 52 * 1024 * 1024


def _params(sem):
    return pltpu.CompilerParams(dimension_semantics=sem, vmem_limit_bytes=VMEM_LIMIT)


def _const_spec(shape):
    nd = len(shape)
    return pl.BlockSpec(shape, lambda *_: (0,) * nd, pipeline_mode=pl.Buffered(1))


def _dot(a, b):
    return jnp.dot(a.astype(BF16), b.astype(BF16), preferred_element_type=F32)


def _dot_nt(a, b):
    return lax.dot_general(a.astype(BF16), b.astype(BF16), (((1,), (1,)), ((), ())),
                           preferred_element_type=F32)


def _dot_tn(a, b):
    return lax.dot_general(a.astype(BF16), b.astype(BF16), (((0,), (0,)), ((), ())),
                           preferred_element_type=F32)


def _split(a):
    hi = a.astype(BF16)
    lo = (a - hi.astype(F32)).astype(BF16)
    return hi, lo


def _dot_hl(a, b01):
    hi, lo = _split(a)
    return (jnp.dot(hi, b01, preferred_element_type=F32)
            + jnp.dot(lo, b01, preferred_element_type=F32))


def _dot_lh(a01, b):
    hi, lo = _split(b)
    return (jnp.dot(a01, hi, preferred_element_type=F32)
            + jnp.dot(a01, lo, preferred_element_type=F32))


def _softplus(x):
    return jnp.maximum(x, 0.0) + jnp.log1p(jnp.exp(-jnp.abs(x)))


def _norm_mod(x, g, scale, shift):
    ms = jnp.mean(x * x, axis=-1, keepdims=True)
    return (x * lax.rsqrt(ms + RMS_EPS) * g) * (1.0 + scale) + shift


def _row_tile(rows, want):
    return want if rows % want == 0 else rows


def _mod_spec(mod, tm):
    _, rm, d = mod.shape
    if rm == 1:
        return pl.BlockSpec((1, 1, d), lambda g, i: (g, 0, 0))
    return pl.BlockSpec((1, tm, d), lambda g, i: (g, i, 0))


def _ada_kernel(c_ref, w_ref, b_ref, o_ref):
    c = c_ref[...]
    o_ref[0] = _dot(c * jax.nn.sigmoid(c), w_ref[0]) + b_ref[0]


def _ada(c_all, w_ada, b_ada):
    depth, d, n = w_ada.shape
    rows = c_all.shape[0]
    tn = 1024
    return pl.pallas_call(
        _ada_kernel,
        grid=(depth, n // tn),
        in_specs=[pl.BlockSpec((rows, d), lambda l, j: (0, 0)),
                  pl.BlockSpec((1, d, tn), lambda l, j: (l, 0, j)),
                  pl.BlockSpec((1, 1, tn), lambda l, j: (l, 0, j))],
        out_specs=pl.BlockSpec((1, rows, tn), lambda l, j: (l, 0, j)),
        out_shape=jax.ShapeDtypeStruct((depth, rows, n), F32),
        compiler_params=_params(("parallel", "parallel")),
        name="adaln",
    )(c_all, w_ada, b_ada.reshape(depth, 1, n))


def _ffn_kernel(x_ref, sh_ref, sc_ref, gt_ref, g_ref, wup_ref, wdn_ref, *rest, dff, chunk, final):
    if final:
        nf_ref, o_ref, acc_ref = rest
    else:
        o_ref, acc_ref = rest
    x = x_ref[0]
    h = _norm_mod(x, g_ref[...], sc_ref[0], sh_ref[0]).astype(BF16)
    for j in range(dff // chunk):
        c0, c1 = j * chunk, (j + 1) * chunk
        g = jnp.dot(h, wup_ref[:, c0:c1], preferred_element_type=F32)
        u = jnp.dot(h, wup_ref[:, dff + c0:dff + c1], preferred_element_type=F32)
        act = (g * jax.nn.sigmoid(g) * u).astype(BF16)
        d = jnp.dot(act, wdn_ref[c0:c1, :], preferred_element_type=F32)
        if j == 0:
            acc_ref[...] = d
        else:
            acc_ref[...] += d
    y = x + HALF_STEP * gt_ref[0] * acc_ref[...]
    if final:
        ms = jnp.mean(y * y, axis=-1, keepdims=True)
        y = y * lax.rsqrt(ms + RMS_EPS) * nf_ref[...]
    o_ref[0] = y


def _ffn(x, shift, scale, gate, g, w_up, w_down, norm_f=None):
    G, R, D = x.shape
    dff = w_down.shape[0]
    tm = _row_tile(R, 512)
    final = norm_f is not None
    in_specs = [pl.BlockSpec((1, tm, D), lambda gi, i: (gi, i, 0)),
                _mod_spec(shift, tm), _mod_spec(scale, tm), _mod_spec(gate, tm),
                _const_spec((1, D)), _const_spec(w_up.shape), _const_spec(w_down.shape)]
    args = [x, shift, scale, gate, g.reshape(1, D), w_up, w_down]
    if final:
        in_specs.append(_const_spec((1, D)))
        args.append(norm_f.reshape(1, D))
    return pl.pallas_call(
        functools.partial(_ffn_kernel, dff=dff, chunk=MXU_WIDTH, final=final),
        grid=(G, R // tm),
        in_specs=in_specs,
        out_specs=pl.BlockSpec((1, tm, D), lambda gi, i: (gi, i, 0)),
        out_shape=jax.ShapeDtypeStruct((G, R, D), F32),
        scratch_shapes=[pltpu.VMEM((tm, D), F32)],
        compiler_params=_params(("parallel", "parallel")),
        name="ffn",
    )(*args)


def _inproj_kernel(x_ref, sh_ref, sc_ref, g_ref, w_ref, *o_refs, widths, chunk):
    h = _norm_mod(x_ref[0], g_ref[...], sc_ref[0], sh_ref[0]).astype(BF16)
    off = 0
    for o_ref, wd in zip(o_refs, widths):
        for c0 in range(0, wd, chunk):
            o_ref[0, :, c0:c0 + chunk] = jnp.dot(h, w_ref[:, off + c0:off + c0 + chunk],
                                                 preferred_element_type=F32)
        off += wd


def _inproj(x, shift, scale, g, w_in, widths):
    G, R, D = x.shape
    tm = _row_tile(R, 256)
    return pl.pallas_call(
        functools.partial(_inproj_kernel, widths=widths, chunk=MXU_WIDTH),
        grid=(G, R // tm),
        in_specs=[pl.BlockSpec((1, tm, D), lambda gi, i: (gi, i, 0)),
                  _mod_spec(shift, tm), _mod_spec(scale, tm),
                  _const_spec((1, D)), _const_spec(w_in.shape)],
        out_specs=[pl.BlockSpec((1, tm, wd), lambda gi, i: (gi, i, 0)) for wd in widths],
        out_shape=[jax.ShapeDtypeStruct((G, R, wd), F32) for wd in widths],
        compiler_params=_params(("parallel", "parallel")),
        name="inproj",
    )(x, shift, scale, g.reshape(1, D), w_in)


def _merge_kernel(x_ref, gt_ref, y0_ref, y1_ref, y2_ref, gates_ref, wb_ref, wo_ref, o_ref, *, d, bw):
    merged = None
    for i, y_ref in enumerate((y0_ref, y1_ref, y2_ref)):
        t = jax.nn.sigmoid(gates_ref[0, :, i * d:(i + 1) * d]) * _dot(y_ref[0], wb_ref[i * bw:(i + 1) * bw, :])
        merged = t if merged is None else merged + t
    o_ref[0] = x_ref[0] + gt_ref[0] * _dot(merged, wo_ref[...])


def _merge(x, gate, y_s5, y_rw, y_sb, gates, w_branch, w_out):
    G, R, D = x.shape
    bw = y_s5.shape[-1]
    tm = _row_tile(R, 512)
    row = lambda wd: pl.BlockSpec((1, tm, wd), lambda gi, i: (gi, i, 0))
    return pl.pallas_call(
        functools.partial(_merge_kernel, d=D, bw=bw),
        grid=(G, R // tm),
        in_specs=[row(D), _mod_spec(gate, tm), row(bw), row(bw), row(bw), row(3 * D),
                  _const_spec(w_branch.shape), _const_spec(w_out.shape)],
        out_specs=row(D),
        out_shape=jax.ShapeDtypeStruct((G, R, D), F32),
        compiler_params=_params(("parallel", "parallel")),
        name="merge",
    )(x, gate, y_s5, y_rw, y_sb, gates, w_branch, w_out)


S5_POWERS = SUBLANES


def _s5_param_kernel(lr_ref, li_ref, ldt_ref, br_ref, bi_ref, bbr_ref, bbi_ref, pwr_ref, pwi_ref):
    lr, li = lr_ref[...], li_ref[...]
    dt = jnp.exp(ldt_ref[...])
    mag = jnp.exp(lr * dt)
    ab_re, ab_im = mag * jnp.cos(li * dt), mag * jnp.sin(li * dt)
    den = lr * lr + li * li
    nr, ni = ab_re - 1.0, ab_im
    f_re = (nr * lr + ni * li) / den
    f_im = (ni * lr - nr * li) / den
    br, bi = br_ref[...], bi_ref[...]
    bbr_ref[...] = f_re * br - f_im * bi
    bbi_ref[...] = f_re * bi + f_im * br
    pr, pi = ab_re, ab_im
    for j in range(S5_POWERS):
        pwr_ref[j] = pr
        pwi_ref[j] = pi
        pr, pi = pr * ab_re - pi * ab_im, pr * ab_im + pi * ab_re


def _s5_params(lam_re, lam_im, log_dt, b_re, b_im):
    G, P = lam_re.shape
    C = b_re.shape[-1]
    sds = jax.ShapeDtypeStruct
    return pl.pallas_call(
        _s5_param_kernel,
        out_shape=[sds((G, C, P), F32), sds((G, C, P), F32),
                   sds((S5_POWERS, G, 1, P), F32), sds((S5_POWERS, G, 1, P), F32)],
        name="s5_params",
    )(lam_re.reshape(G, 1, P), lam_im.reshape(G, 1, P),
      jnp.broadcast_to(log_dt[:, None, None], (G, 1, P)),
      jnp.swapaxes(b_re, 1, 2), jnp.swapaxes(b_im, 1, 2))


S5_LANE_CHUNK = 256


def _s5_kernel(u_ref, x0r_ref, x0i_ref, bbr_ref, bbi_ref, cr_ref, ci_ref, d_ref, wg_ref, bg_ref,
               pwr_ref, pwi_ref, y_ref, xr_out, xi_out, sr_ref, si_ref, car_ref, cai_ref,
               *, tl, nl, last_row):
    li = pl.program_id(1)

    @pl.when(li == 0)
    def _():
        car_ref[...] = x0r_ref[0]
        cai_ref[...] = x0i_ref[0]

    u = u_ref[0]
    ub = u.astype(BF16)
    n_tiles, wi, ws = bbr_ref.shape
    for g in range(n_tiles):
        ug = ub[:, g * wi:(g + 1) * wi]
        sr_ref[:, g * ws:(g + 1) * ws] = jnp.dot(ug, bbr_ref[g], preferred_element_type=F32)
        si_ref[:, g * ws:(g + 1) * ws] = jnp.dot(ug, bbi_ref[g], preferred_element_type=F32)
    n_state = sr_ref.shape[1]
    lc = S5_LANE_CHUNK
    row = lax.broadcasted_iota(jnp.int32, (SUBLANES, lc), 0)
    for c in range(n_state // lc):
        ls = slice(c * lc, (c + 1) * lc)
        pr8, pi8 = pwr_ref[:, ls], pwi_ref[:, ls]
        steps = []
        for dd in (1, 2, 4):
            steps.append((dd, jnp.where(row >= dd, pr8[dd - 1:dd, :], 0.0),
                          jnp.where(row >= dd, pi8[dd - 1:dd, :], 0.0)))

        def body(i, carry, ls=ls, pr8=pr8, pi8=pi8, steps=steps):
            car, cai = carry
            r0 = pl.multiple_of(i * SUBLANES, SUBLANES)
            xr = sr_ref[pl.ds(r0, SUBLANES), ls]
            xi = si_ref[pl.ds(r0, SUBLANES), ls]
            for dd, ar, ai in steps:
                rr = pltpu.roll(xr, dd, 0)
                ri = pltpu.roll(xi, dd, 0)
                xr, xi = xr + ar * rr - ai * ri, xi + ar * ri + ai * rr
            xr, xi = xr + pr8 * car - pi8 * cai, xi + pr8 * cai + pi8 * car
            sr_ref[pl.ds(r0, SUBLANES), ls] = xr
            si_ref[pl.ds(r0, SUBLANES), ls] = xi
            return xr[SUBLANES - 1:SUBLANES, :], xi[SUBLANES - 1:SUBLANES, :]

        car, cai = lax.fori_loop(0, tl // SUBLANES, body, (car_ref[:, ls], cai_ref[:, ls]))
        car_ref[:, ls] = car
        cai_ref[:, ls] = cai

    y = jnp.concatenate([_dot(sr_ref[:, g * ws:(g + 1) * ws], cr_ref[g]) - _dot(si_ref[:, g * ws:(g + 1) * ws], ci_ref[g])
                         for g in range(n_tiles)], axis=1) + d_ref[...] * u
    z = jax.nn.gelu(y)
    y_ref[0] = z * jax.nn.sigmoid(_dot(z, wg_ref[...]) + bg_ref[...])

    @pl.when(li == nl - 1)
    def _():
        xr_out[0] = sr_ref[last_row:last_row + 1, :]
        xi_out[0] = si_ref[last_row:last_row + 1, :]


def _s5(u, x0_re, x0_im, prm, seq_len):
    B, Lp, W = u.shape
    n_state = prm["bb_re"].shape[0] * prm["bb_re"].shape[2]
    tl = _row_tile(Lp, 256)
    nl = Lp // tl
    last_row = (seq_len - 1) - (nl - 1) * tl
    assert 0 <= last_row < tl
    state = pl.BlockSpec((1, 1, n_state), lambda b, i: (b, 0, 0))
    sds = jax.ShapeDtypeStruct
    return pl.pallas_call(
        functools.partial(_s5_kernel, tl=tl, nl=nl, last_row=last_row),
        grid=(B, nl),
        in_specs=[pl.BlockSpec((1, tl, W), lambda b, i: (b, i, 0)), state, state,
                  _const_spec(prm["bb_re"].shape), _const_spec(prm["bb_im"].shape),
                  _const_spec(prm["c_re"].shape), _const_spec(prm["c_im"].shape),
                  _const_spec((1, W)), _const_spec((W, W)), _const_spec((1, W)),
                  _const_spec((S5_POWERS, n_state)), _const_spec((S5_POWERS, n_state))],
        out_specs=[pl.BlockSpec((1, tl, W), lambda b, i: (b, i, 0)), state, state],
        out_shape=[sds((B, Lp, W), F32), sds((B, 1, n_state), F32), sds((B, 1, n_state), F32)],
        scratch_shapes=[pltpu.VMEM((tl, n_state), F32), pltpu.VMEM((tl, n_state), F32),
                        pltpu.VMEM((1, n_state), F32), pltpu.VMEM((1, n_state), F32)],
        compiler_params=_params(("parallel", "arbitrary")),
        name="s5",
    )(u, x0_re, x0_im, prm["bb_re"], prm["bb_im"], prm["c_re"], prm["c_im"], prm["d"],
      prm["w_glu"], prm["b_glu"], prm["pw_re"], prm["pw_im"])


def _rwkv_kernel(p_ref, sh0_ref, s0_ref, mu_ref, w0_ref, w2_ref, a0_ref, a2_ref, g2_ref, kk_ref, ka_ref,
                 rk_ref, gnw_ref, gnb_ref, seg_ref, tri_ref, y_ref, sfin_ref, S_ref, prev_ref,
                 *, T, nl, valid, heads):
    li = pl.program_id(1)
    W = heads * HEAD_DIM

    @pl.when(li == 0)
    def _():
        S_ref[...] = s0_ref[0]
        prev_ref[...] = sh0_ref[0]

    p = p_ref[0]
    trow = lax.broadcasted_iota(jnp.int32, (T, 1), 0)
    p_prev = jnp.where(trow == 0, prev_ref[...], pltpu.roll(p, 1, 0))
    prev_ref[...] = p[T - 1:T, :]
    ps = p + (p_prev - p) * mu_ref[...]
    r, k, v = ps[:, 0:W], ps[:, W:2 * W], ps[:, 2 * W:3 * W]
    o3 = 3 * W
    dw = ps[:, o3:o3 + RW_DECAY_RANK]
    da = ps[:, o3 + RW_DECAY_RANK:o3 + RW_DECAY_RANK + RW_AICL_RANK]
    dg = ps[:, o3 + RW_DECAY_RANK + RW_AICL_RANK:]
    w_log = -_softplus(-(w0_ref[...] + _dot(jnp.tanh(dw), w2_ref[...]))) - 0.5
    lw = -jnp.exp(w_log)
    a = jax.nn.sigmoid(a0_ref[...] + _dot(da, a2_ref[...]))
    g = _dot(jax.nn.sigmoid(dg), g2_ref[...])
    seg = seg_ref[...]
    kk = k * kk_ref[...]
    kk = kk / jnp.maximum(jnp.sqrt(_dot_hl(kk * kk, seg)), 1e-12)
    km = k * (1.0 + (a - 1.0) * ka_ref[...])
    bonus = _dot_hl(r * km * rk_ref[...], seg)
    if valid < T:
        live = trow < valid
        lw = jnp.where(live, lw, 0.0)
        kk = jnp.where(live, kk, 0.0)
        km = jnp.where(live, km, 0.0)

    cum = _dot_lh(tri_ref[...], lw)
    c_end = cum[T - 1:T, :]
    e_neg = jnp.exp(-cum)
    e_end = jnp.exp(c_end - cum)
    kka = kk * a
    A = -kk * jnp.exp(cum - lw)
    Bt = kka * e_neg
    Kt = km * e_neg
    Rt = r * jnp.exp(cum)
    Bh = kka * e_end
    Kh = km * e_end
    w_end = jnp.exp(c_end)

    ti = lax.broadcasted_iota(jnp.int32, (T, T), 0)
    tj = lax.broadcasted_iota(jnp.int32, (T, T), 1)
    strict = ti > tj
    incl = ti >= tj
    eye = jnp.where(ti == tj, 1.0, 0.0)
    H = range(heads)
    hsl = [slice(h * HEAD_DIM, (h + 1) * HEAD_DIM) for h in H]
    Ss = [S_ref[h] for h in H]
    ARs = [jnp.concatenate([A[:, hs], Rt[:, hs]], axis=0) for hs in hsl]
    BKs = [jnp.concatenate([Bt[:, hs], Kt[:, hs]], axis=0) for hs in hsl]
    Gms = [_dot_nt(ARs[h], BKs[h]) for h in H]
    ARSs = [_dot_nt(ARs[h], Ss[h]) for h in H]
    Ns = [jnp.where(strict, Gm[0:T, 0:T], 0.0) for Gm in Gms]
    Ms = [jnp.where(strict, Gm[0:T, T:2 * T], 0.0) for Gm in Gms]
    Prbs = [jnp.where(incl, Gm[T:2 * T, 0:T], 0.0) for Gm in Gms]
    Prks = [jnp.where(incl, Gm[T:2 * T, T:2 * T], 0.0) for Gm in Gms]
    vhs = [v[:, hs] for hs in hsl]
    MVs = [_dot(Ms[h], vhs[h]) for h in H]
    Xs = [eye + N for N in Ns]
    Ps = Ns
    n = 1
    while 2 * n < T:
        Ps = [_dot(P, P) for P in Ps]
        Xs = [Xs[h] + _dot(Xs[h], Ps[h]) for h in H]
        n *= 2
    Us = [_dot(Xs[h], ARSs[h][0:T] + MVs[h]) for h in H]
    Os = [ARSs[h][T:2 * T] + _dot(Prbs[h], Us[h]) + _dot(Prks[h], vhs[h]) for h in H]
    for h in H:
        hs = hsl[h]
        S_ref[h] = Ss[h] * w_end[:, hs] + _dot_tn(Us[h], Bh[:, hs]) + _dot_tn(vhs[h], Kh[:, hs])
    for h in H:
        hs = hsl[h]
        O = Os[h]
        mean = jnp.mean(O, axis=-1, keepdims=True)
        var = jnp.mean(jnp.square(O - mean), axis=-1, keepdims=True)
        on = (O - mean) * lax.rsqrt(var + RW_GN_EPS) * gnw_ref[:, hs] + gnb_ref[:, hs]
        on = on + bonus[:, hs] * vhs[h]
        y_ref[0, :, hs] = on * g[:, hs]

    @pl.when(li == nl - 1)
    def _():
        sfin_ref[0] = S_ref[...]


def _rwkv(p, shift0, s0, prm, seq_len, T):
    B, Lp, NP = p.shape
    heads = s0.shape[1]
    W = heads * HEAD_DIM
    nl = Lp // T
    valid = seq_len - (nl - 1) * T
    assert nl * T == Lp and 0 < valid <= T and (nl == 1 or valid == T)
    tri = jnp.tril(jnp.ones((T, T), BF16))
    vec = lambda n: _const_spec((1, n))
    sds = jax.ShapeDtypeStruct
    st_spec = pl.BlockSpec((1, heads, HEAD_DIM, HEAD_DIM), lambda b, i: (b, 0, 0, 0))
    return pl.pallas_call(
        functools.partial(_rwkv_kernel, T=T, nl=nl, valid=valid, heads=heads),
        grid=(B, nl),
        in_specs=[pl.BlockSpec((1, T, NP), lambda b, i: (b, i, 0)),
                  pl.BlockSpec((1, 1, NP), lambda b, i: (b, 0, 0)), st_spec,
                  vec(NP), vec(W), _const_spec((RW_DECAY_RANK, W)), vec(W), _const_spec((RW_AICL_RANK, W)),
                  _const_spec((RW_GATE_RANK, W)), vec(W), vec(W), vec(W), vec(W), vec(W),
                  _const_spec((W, W)), _const_spec((T, T))],
        out_specs=[pl.BlockSpec((1, T, W), lambda b, i: (b, i, 0)), st_spec],
        out_shape=[sds((B, Lp, W), F32), sds(s0.shape, F32)],
        scratch_shapes=[pltpu.VMEM((heads, HEAD_DIM, HEAD_DIM), F32), pltpu.VMEM((1, NP), F32)],
        compiler_params=_params(("parallel", "arbitrary")),
        name="rwkv",
    )(p, shift0, s0, prm["mu"], prm["w0"], prm["w2"], prm["a0"], prm["a2"], prm["g2"], prm["k_k"],
      prm["k_a"], prm["r_k"], prm["gn_w"], prm["gn_b"], prm["seg"], tri)


SB_TILE = MXU_WIDTH
SB_Q_SUB = 2
SB_CAST_ROWS = 512
SB_BIAS_SPLIT = 3
LOG2E = math.log2(math.e)


def _sb_tiles(qs, kvs, carries, masks, upper):
    C, J = range(len(qs)), range(len(kvs))
    nzs = [[_dot_nt(qs[c], kvs[j][c][0]) for c in C] for j in J]
    ms = [[jnp.minimum(nzs[j][c], 0.0) for c in C] for j in J]
    ps = [[ms[j][c] - nzs[j][c] for c in C] for j in J]
    ls = [[jnp.log2(1.0 + jnp.exp2(ms[j][c] + ps[j][c])) for c in C] for j in J]
    lks = [[ms[j][c] - ls[j][c] for c in C] for j in J]
    lks = [[lks[j][c] if masks[j][c] is None else jnp.where(masks[j][c], lks[j][c], 0.0) for c in C] for j in J]
    cin = [list(carries)]
    for j in J:
        cin.append([cin[j][c] + jnp.sum(lks[j][c], axis=-1, keepdims=True) for c in C])
    css = [[_dot(lks[j][c], upper) + cin[j][c] for c in C] for j in J]
    ws = [[jnp.exp2(ps[j][c] - ls[j][c] + css[j][c]) for c in C] for j in J]
    ws = [[ws[j][c] if masks[j][c] is None else jnp.where(masks[j][c], ws[j][c], 0.0) for c in C] for j in J]
    ds = [[_dot(ws[j][c], kvs[j][c][1]) for c in C] for j in J]
    out = ds[0]
    for j in J[1:]:
        out = [out[c] + ds[j][c] for c in C]
    return out, cin[-1]


SB_K_PER_ITER = 2


def _sb_kernel(q_ref, k_ref, v_ref, bias_ref, up_ref, o_ref, kb_ref, vb_ref, *, t, nsub, scale, pair, L):
    qi = pl.program_id(2)
    heads = range(pair)
    lane = lax.broadcasted_iota(jnp.int32, (1, LANES), 1)
    is_data = lane < HEAD_DIM

    def head_lanes(x, hh):
        return x if hh == 0 else pltpu.roll(x, LANES - hh * HEAD_DIM, 1)

    @pl.when(qi == 0)
    def _():
        rows = min(SB_CAST_ROWS, L)
        ones = jnp.where(lane < HEAD_DIM + SB_BIAS_SPLIT, 1.0, 0.0)

        def cast(c, _):
            r0 = pl.multiple_of(c * rows, rows)
            kk = k_ref[0, pl.ds(r0, rows), :]
            vv = v_ref[0, pl.ds(r0, rows), :]
            for hh in heads:
                kb_ref[hh, pl.ds(r0, rows), :] = jnp.where(is_data, head_lanes(kk, hh), ones).astype(BF16)
                vb_ref[hh, pl.ds(r0, rows), :] = head_lanes(vv, hh)[:, 0:HEAD_DIM].astype(BF16)
            return 0

        lax.fori_loop(0, L // rows, cast, 0)

    q2 = q_ref[0] * (-scale * LOG2E)
    qs = [jnp.where(is_data, head_lanes(q2[s * t:(s + 1) * t, :], hh), bias_ref[0, hh:hh + 1, :]).astype(BF16)
          for s in range(nsub) for hh in heads]
    upper = up_ref[...]
    ri = lax.broadcasted_iota(jnp.int32, (t, t), 0)
    ci = lax.broadcasted_iota(jnp.int32, (t, t), 1)
    diag_mask = ci < ri

    def kv_tile(k0, chains):
        return [(kb_ref[c % pair, pl.ds(k0, t), :], vb_ref[c % pair, pl.ds(k0, t), :]) for c in chains]

    zero = jnp.zeros((t, 1), F32)
    accs = [None] * (nsub * pair)
    carries = [zero] * (nsub * pair)
    for kt in reversed(range(nsub)):
        chains = list(range(kt * pair, nsub * pair))
        k0 = pl.multiple_of((qi * nsub + kt) * t, t)
        ds, cs = _sb_tiles([qs[c] for c in chains], [kv_tile(k0, chains)], [carries[c] for c in chains],
                           [[diag_mask if c // pair == kt else None for c in chains]], upper)
        for i, c in enumerate(chains):
            accs[c] = ds[i] if accs[c] is None else accs[c] + ds[i]
            carries[c] = cs[i]

    chains = list(range(nsub * pair))
    n_tiles = qi * nsub
    kpi = SB_K_PER_ITER if nsub % SB_K_PER_ITER == 0 else 1
    no_mask = [[None] * len(chains)] * kpi

    def body(it, st):
        accs, carries = st
        k0s = [pl.multiple_of((n_tiles - 1 - it * kpi - j) * t, t) for j in range(kpi)]
        ds, carries = _sb_tiles(qs, [kv_tile(k0, chains) for k0 in k0s], list(carries), no_mask, upper)
        return tuple(a + d for a, d in zip(accs, ds)), tuple(carries)

    accs, _ = lax.fori_loop(0, n_tiles // kpi, body, (tuple(accs), tuple(carries)))
    for c in chains:
        s, hh = divmod(c, pair)
        o_ref[0, s * t:(s + 1) * t, hh * HEAD_DIM:(hh + 1) * HEAD_DIM] = accs[c]


def _sb_prompt(q, k, v, bias):
    B, L, W = q.shape
    H = W // HEAD_DIM
    pair = LANES // HEAD_DIM
    t = _row_tile(L, SB_TILE)
    nsub = SB_Q_SUB if L % (SB_Q_SUB * t) == 0 else 1
    upper = jnp.triu(jnp.ones((t, t), BF16), 1).T
    rest = -bias * LOG2E
    parts = []
    for _ in range(SB_BIAS_SPLIT):
        parts.append(rest.astype(BF16).astype(F32))
        rest = rest - parts[-1]
    bias_rows = jnp.zeros((H, LANES), F32).at[:, HEAD_DIM:HEAD_DIM + SB_BIAS_SPLIT].set(jnp.stack(parts, axis=1))
    qspec = pl.BlockSpec((1, nsub * t, LANES), lambda b, hp, i: (b, i, hp))
    kvspec = pl.BlockSpec((1, L, LANES), lambda b, hp, i: (b, 0, hp))
    return pl.pallas_call(
        functools.partial(_sb_kernel, t=t, nsub=nsub, scale=HEAD_DIM ** -0.5, pair=pair, L=L),
        grid=(B, W // LANES, L // (nsub * t)),
        in_specs=[qspec, kvspec, kvspec, pl.BlockSpec((1, pair, LANES), lambda b, hp, i: (hp, 0, 0)),
                  _const_spec((t, t))],
        out_specs=qspec,
        out_shape=jax.ShapeDtypeStruct((B, L, W), F32),
        scratch_shapes=[pltpu.VMEM((pair, L, LANES), BF16), pltpu.VMEM((pair, L, HEAD_DIM), BF16)],
        compiler_params=_params(("parallel", "parallel", "arbitrary")),
        name="sb_prompt",
    )(q, k, v, bias_rows.reshape(H // pair, pair, LANES), upper)


SB_DEC_PAGES = 8


def _sb_decode_kernel(pt_ref, q_ref, nbias_ref, *refs, pps, scale):
    k_refs, v_refs = refs[:pps], refs[pps:2 * pps]
    o_ref, acc_ref, carry_ref = refs[2 * pps:]
    step = pl.program_id(1)
    H, d, page = acc_ref.shape

    @pl.when(step == 0)
    def _():
        acc_ref[...] = jnp.zeros_like(acc_ref)
        carry_ref[...] = jnp.zeros_like(carry_ref)

    qb = jnp.broadcast_to(q_ref[0] * (-scale * LOG2E), (H, d, page))
    lane = lax.broadcasted_iota(jnp.int32, (H, 1, page), 2)
    pages = range(pps)
    nzs = [jnp.sum(k_refs[i][0, 0] * qb, axis=1, keepdims=True) + nbias_ref[...] for i in pages]
    lks = [jnp.minimum(nz, 0.0) - jnp.log2(1.0 + jnp.exp2(-jnp.abs(nz))) for nz in nzs]
    sufs = list(lks)
    sh = 1
    while sh < page:
        sufs = [x + jnp.where(lane < page - sh, pltpu.roll(x, page - sh, 2), 0.0) for x in sufs]
        sh *= 2
    carry = carry_ref[...]
    css = []
    for i in pages:
        css.append(sufs[i] - lks[i] + carry)
        carry = carry + sufs[i][:, :, 0:1]
    carry_ref[...] = carry
    ws = [jnp.exp2(lks[i] - nzs[i] + css[i]) for i in pages]
    acc = acc_ref[...]
    for i in pages:
        acc = acc + v_refs[i][0, 0] * ws[i]
    acc_ref[...] = acc

    @pl.when(step == pl.num_programs(1) - 1)
    def _():
        o_ref[0] = jnp.sum(acc, axis=-1, keepdims=True)


def _sb_decode(q, cache_k, cache_v, layer, page_table, bias):
    B, _, W = q.shape
    n_pages = page_table.shape[1]
    depth, n_pool, page, H, d = cache_k.shape
    pps = SB_DEC_PAGES if n_pages % SB_DEC_PAGES == 0 else 1
    ck = jnp.transpose(cache_k, (0, 1, 3, 4, 2))
    cv = jnp.transpose(cache_v, (0, 1, 3, 4, 2))
    nbias = jnp.broadcast_to((-bias * LOG2E)[:, None, None], (H, 1, page))
    kv = [pl.BlockSpec((1, 1, H, d, page),
                       lambda b, s, pt, i=i: (layer, pt[b, n_pages - 1 - (s * pps + i)], 0, 0, 0))
          for i in range(pps)]
    qo = pl.BlockSpec((1, H, d, 1), lambda b, s, pt: (b, 0, 0, 0))
    out = pl.pallas_call(
        functools.partial(_sb_decode_kernel, pps=pps, scale=d ** -0.5),
        grid_spec=pltpu.PrefetchScalarGridSpec(
            num_scalar_prefetch=1,
            grid=(B, n_pages // pps),
            in_specs=[qo, pl.BlockSpec((H, 1, page), lambda b, s, pt: (0, 0, 0))] + kv + kv,
            out_specs=qo,
            scratch_shapes=[pltpu.VMEM((H, d, page), F32), pltpu.VMEM((H, 1, page), F32)]),
        out_shape=jax.ShapeDtypeStruct((B, H, d, 1), F32),
        compiler_params=_params(("parallel", "arbitrary")),
        name="sb_decode",
    )(page_table, q.reshape(B, H, d, 1), nbias, *([ck] * pps), *([cv] * pps))
    return out.reshape(B, 1, W)


def _block_diag(m):
    G, a, b = m.shape
    per = LANES // min(a, b)
    eye = jnp.eye(per, dtype=m.dtype)
    mt = m.reshape(G // per, per, a, b)
    return (mt[:, :, :, None, :] * eye[None, :, None, :, None]).reshape(G // per, per * a, per * b)


def _layer(x, mod, P, seq_len, s5_0, rw_s0, rw_shift0, sb_fn, norm_f):
    G, R, D = x.shape
    m = lambda i, j: mod[:, :, i, j, :]
    x = _ffn(x, m(0, 0), m(0, 1), m(0, 2), P["norm_g"][0], P["w_up"][0], P["w_down"][0])
    u, p, q, k, v, gates = _inproj(x, m(1, 0), m(1, 1), P["norm_g"][1], P["w_in"], P["widths"])
    B = G * R // seq_len
    seqs = lambda t: t.reshape(B, seq_len, t.shape[-1])
    pad = (-seq_len) % SUBLANES

    def padded(t):
        t = seqs(t)
        return jnp.pad(t, ((0, 0), (0, pad), (0, 0))) if pad else t

    y_s5, s5_re, s5_im = _s5(padded(u), s5_0[0], s5_0[1], P["s5"], seq_len)
    T = 64 if seq_len % 64 == 0 else SUBLANES
    y_rw, rw_s = _rwkv(padded(p), rw_shift0, rw_s0, P["rw"], seq_len, T)
    y_sb = sb_fn(seqs(q), seqs(k), seqs(v))
    rows = lambda t: t[:, :seq_len].reshape(G, R, t.shape[-1])
    x = _merge(x, m(1, 2), rows(y_s5), rows(y_rw), rows(y_sb), gates, P["w_branch"], P["w_out"])
    x = _ffn(x, m(2, 0), m(2, 1), m(2, 2), P["norm_g"][2], P["w_up"][1], P["w_down"][1], norm_f)
    return x, (seqs(k), seqs(v), s5_re, s5_im, rw_s, seqs(p)[:, -1])


def kernel(x_prompt, x_sample, cache_k, cache_v, state_s5_re, state_s5_im, state_rwkv, state_rwkv_shift, page_table, c_prompt, c_sample, norm_g, w_ada, b_ada, w_ffn_up, w_ffn_down, w_in, s5_lam_re, s5_lam_im, s5_log_dt, s5_b_re, s5_b_im, s5_c_re, s5_c_im, s5_d, s5_w_glu, s5_b_glu, rw_mu, rw_w0, rw_w2, rw_a0, rw_a2, rw_g2, rw_k_k, rw_k_a, rw_r_k, rw_gn_w, rw_gn_b, sb_bias, w_branch, w_out, norm_f):
    bp, L, D = x_prompt.shape
    bs = x_sample.shape[0]
    depth = w_in.shape[0]
    G5, P5 = s5_lam_re.shape[1:]
    n_state = G5 * P5
    s5_w = G5 * S5_GROUP
    heads = state_rwkv.shape[2]
    W = heads * HEAD_DIM
    n_proj = state_rwkv_shift.shape[-1]
    widths = (s5_w, n_proj, W, W, W, 3 * D)

    c_all = jnp.concatenate([c_prompt, c_sample], axis=0)
    c_all = jnp.pad(c_all, ((0, (-c_all.shape[0]) % SUBLANES), (0, 0)))
    mod_all = _ada(c_all, w_ada, b_ada)
    mod_p = mod_all[:, :bp].reshape(depth, bp, 1, 3, 3, D)
    mod_s = mod_all[:, bp:bp + bs].reshape(depth, 1, bs, 3, 3, D)

    seg = (jnp.arange(W)[:, None] // HEAD_DIM == jnp.arange(W)[None, :] // HEAD_DIM).astype(BF16)
    xp = x_prompt
    xs = x_sample.reshape(1, bs, D)
    outs_p, outs_s = [], []
    for l in range(depth):
        bb_re, bb_im, pw_re, pw_im = _s5_params(s5_lam_re[l], s5_lam_im[l], s5_log_dt[l], s5_b_re[l], s5_b_im[l])
        P = dict(
            norm_g=norm_g[l], w_up=w_ffn_up[l].astype(BF16), w_down=w_ffn_down[l].astype(BF16),
            w_in=w_in[l].astype(BF16), widths=widths,
            w_branch=w_branch[l].astype(BF16), w_out=w_out[l].astype(BF16),
            s5=dict(bb_re=_block_diag(bb_re).astype(BF16), bb_im=_block_diag(bb_im).astype(BF16),
                    c_re=_block_diag(jnp.swapaxes(s5_c_re[l], 1, 2)).astype(BF16),
                    c_im=_block_diag(jnp.swapaxes(s5_c_im[l], 1, 2)).astype(BF16),
                    d=s5_d[l].reshape(1, s5_w), w_glu=s5_w_glu[l].astype(BF16), b_glu=s5_b_glu[l].reshape(1, s5_w),
                    pw_re=pw_re.reshape(S5_POWERS, n_state), pw_im=pw_im.reshape(S5_POWERS, n_state)),
            rw=dict(mu=rw_mu[l].reshape(1, n_proj), w0=rw_w0[l].reshape(1, W), w2=rw_w2[l].astype(BF16),
                    a0=rw_a0[l].reshape(1, W), a2=rw_a2[l].astype(BF16), g2=rw_g2[l].astype(BF16),
                    k_k=rw_k_k[l].reshape(1, W), k_a=rw_k_a[l].reshape(1, W), r_k=rw_r_k[l].reshape(1, W),
                    gn_w=rw_gn_w[l].reshape(1, W), gn_b=rw_gn_b[l].reshape(1, W), seg=seg))
        nf = norm_f if l == depth - 1 else None
        bias = sb_bias[l]
        zs5 = jnp.zeros((bp, 1, n_state), F32)
        xp, st = _layer(xp, mod_p[l], P, L, (zs5, zs5), jnp.zeros((bp, heads, HEAD_DIM, HEAD_DIM), F32),
                        jnp.zeros((bp, 1, n_proj), F32),
                        lambda q, k, v: _sb_prompt(q, k, v, bias), nf)
        outs_p.append(st)
        xs, st = _layer(xs, mod_s[l], P, 1,
                        (state_s5_re[l].reshape(bs, 1, n_state), state_s5_im[l].reshape(bs, 1, n_state)),
                        state_rwkv[l], state_rwkv_shift[l].reshape(bs, 1, n_proj),
                        lambda q, k, v: _sb_decode(q, cache_k, cache_v, l, page_table, bias), nf)
        outs_s.append(st)

    def stacked(outs, i, shape):
        return jnp.stack([o[i] for o in outs]).reshape((depth,) + shape)

    def group(outs, b, seq):
        return (stacked(outs, 0, (b, seq, heads, HEAD_DIM)), stacked(outs, 1, (b, seq, heads, HEAD_DIM)),
                stacked(outs, 2, (b, G5, P5)), stacked(outs, 3, (b, G5, P5)),
                stacked(outs, 4, (b, heads, HEAD_DIM, HEAD_DIM)), stacked(outs, 5, (b, n_proj)))

    return (xp, xs.reshape(bs, 1, D)) + group(outs_p, bp, L) + group(outs_s, bs, 1)
```

```python
import functools
import math

import jax
import jax.numpy as jnp
from jax import lax
from jax.experimental import pallas as pl
from jax.experimental.pallas import tpu as pltpu

F32 = jnp.float32
BF16 = jnp.bfloat16

S5_GROUP = 16
S5_STATE = 64
HEAD_DIM = 64
RW_DECAY_RANK = 64
RW_AICL_RANK = 64
RW_GATE_RANK = 128
RW_GN_EPS = 64e-5
HALF_STEP = 0.5
RMS_EPS = 1e-6
PAGE_SIZE = 128

LANES = 128
SUBLANES = 8
MXU_WIDTH = 256
VMEM_LIMIT = 52 * 1024 * 1024


def _params(sem):
    return pltpu.CompilerParams(dimension_semantics=sem, vmem_limit_bytes=VMEM_LIMIT)


def _const_spec(shape):
    nd = len(shape)
    return pl.BlockSpec(shape, lambda *_: (0,) * nd, pipeline_mode=pl.Buffered(1))


def _dot(a, b):
    return jnp.dot(a.astype(BF16), b.astype(BF16), preferred_element_type=F32)


def _dot_nt(a, b):
    return lax.dot_general(a.astype(BF16), b.astype(BF16), (((1,), (1,)), ((), ())),
                           preferred_element_type=F32)


def _dot_tn(a, b):
    return lax.dot_general(a.astype(BF16), b.astype(BF16), (((0,), (0,)), ((), ())),
                           preferred_element_type=F32)


def _split(a):
    hi = a.astype(BF16)
    lo = (a - hi.astype(F32)).astype(BF16)
    return hi, lo


def _dot_hl(a, b01):
    hi, lo = _split(a)
    return (jnp.dot(hi, b01, preferred_element_type=F32)
            + jnp.dot(lo, b01, preferred_element_type=F32))


def _dot_lh(a01, b):
    hi, lo = _split(b)
    return (jnp.dot(a01, hi, preferred_element_type=F32)
            + jnp.dot(a01, lo, preferred_element_type=F32))


def _softplus(x):
    return jnp.maximum(x, 0.0) + jnp.log1p(jnp.exp(-jnp.abs(x)))


def _norm_mod(x, g, scale, shift):
    ms = jnp.mean(x * x, axis=-1, keepdims=True)
    return (x * lax.rsqrt(ms + RMS_EPS) * g) * (1.0 + scale) + shift


def _row_tile(rows, want):
    return want if rows % want == 0 else rows


def _mod_spec(mod, tm):
    _, rm, d = mod.shape
    if rm == 1:
        return pl.BlockSpec((1, 1, d), lambda g, i: (g, 0, 0))
    return pl.BlockSpec((1, tm, d), lambda g, i: (g, i, 0))


def _ada_kernel(c_ref, w_ref, b_ref, o_ref):
    c = c_ref[...]
    o_ref[0] = _dot(c * jax.nn.sigmoid(c), w_ref[0]) + b_ref[0]


def _ada(c_all, w_ada, b_ada):
    depth, d, n = w_ada.shape
    rows = c_all.shape[0]
    tn = 1024
    return pl.pallas_call(
        _ada_kernel,
        grid=(depth, n // tn),
        in_specs=[pl.BlockSpec((rows, d), lambda l, j: (0, 0)),
                  pl.BlockSpec((1, d, tn), lambda l, j: (l, 0, j)),
                  pl.BlockSpec((1, 1, tn), lambda l, j: (l, 0, j))],
        out_specs=pl.BlockSpec((1, rows, tn), lambda l, j: (l, 0, j)),
        out_shape=jax.ShapeDtypeStruct((depth, rows, n), F32),
        compiler_params=_params(("parallel", "parallel")),
        name="adaln",
    )(c_all, w_ada, b_ada.reshape(depth, 1, n))


def _ffn_kernel(x_ref, sh_ref, sc_ref, gt_ref, g_ref, wup_ref, wdn_ref, *rest, dff, chunk, final):
    if final:
        nf_ref, o_ref, acc_ref = rest
    else:
        o_ref, acc_ref = rest
    x = x_ref[0]
    h = _norm_mod(x, g_ref[...], sc_ref[0], sh_ref[0]).astype(BF16)
    for j in range(dff // chunk):
        c0, c1 = j * chunk, (j + 1) * chunk
        g = jnp.dot(h, wup_ref[:, c0:c1], preferred_element_type=F32)
        u = jnp.dot(h, wup_ref[:, dff + c0:dff + c1], preferred_element_type=F32)
        act = (g * jax.nn.sigmoid(g) * u).astype(BF16)
        d = jnp.dot(act, wdn_ref[c0:c1, :], preferred_element_type=F32)
        if j == 0:
            acc_ref[...] = d
        else:
            acc_ref[...] += d
    y = x + HALF_STEP * gt_ref[0] * acc_ref[...]
    if final:
        ms = jnp.mean(y * y, axis=-1, keepdims=True)
        y = y * lax.rsqrt(ms + RMS_EPS) * nf_ref[...]
    o_ref[0] = y


def _ffn(x, shift, scale, gate, g, w_up, w_down, norm_f=None):
    G, R, D = x.shape
    dff = w_down.shape[0]
    tm = _row_tile(R, 512)
    final = norm_f is not None
    in_specs = [pl.BlockSpec((1, tm, D), lambda gi, i: (gi, i, 0)),
                _mod_spec(shift, tm), _mod_spec(scale, tm), _mod_spec(gate, tm),
                _const_spec((1, D)), _const_spec(w_up.shape), _const_spec(w_down.shape)]
    args = [x, shift, scale, gate, g.reshape(1, D), w_up, w_down]
    if final:
        in_specs.append(_const_spec((1, D)))
        args.append(norm_f.reshape(1, D))
    return pl.pallas_call(
        functools.partial(_ffn_kernel, dff=dff, chunk=MXU_WIDTH, final=final),
        grid=(G, R // tm),
        in_specs=in_specs,
        out_specs=pl.BlockSpec((1, tm, D), lambda gi, i: (gi, i, 0)),
        out_shape=jax.ShapeDtypeStruct((G, R, D), F32),
        scratch_shapes=[pltpu.VMEM((tm, D), F32)],
        compiler_params=_params(("parallel", "parallel")),
        name="ffn",
    )(*args)


def _inproj_kernel(x_ref, sh_ref, sc_ref, g_ref, w_ref, *o_refs, widths, chunk):
    h = _norm_mod(x_ref[0], g_ref[...], sc_ref[0], sh_ref[0]).astype(BF16)
    off = 0
    for o_ref, wd in zip(o_refs, widths):
        for c0 in range(0, wd, chunk):
            o_ref[0, :, c0:c0 + chunk] = jnp.dot(h, w_ref[:, off + c0:off + c0 + chunk],
                                                 preferred_element_type=F32)
        off += wd


def _inproj(x, shift, scale, g, w_in, widths):
    G, R, D = x.shape
    tm = _row_tile(R, 256)
    return pl.pallas_call(
        functools.partial(_inproj_kernel, widths=widths, chunk=MXU_WIDTH),
        grid=(G, R // tm),
        in_specs=[pl.BlockSpec((1, tm, D), lambda gi, i: (gi, i, 0)),
                  _mod_spec(shift, tm), _mod_spec(scale, tm),
                  _const_spec((1, D)), _const_spec(w_in.shape)],
        out_specs=[pl.BlockSpec((1, tm, wd), lambda gi, i: (gi, i, 0)) for wd in widths],
        out_shape=[jax.ShapeDtypeStruct((G, R, wd), F32) for wd in widths],
        compiler_params=_params(("parallel", "parallel")),
        name="inproj",
    )(x, shift, scale, g.reshape(1, D), w_in)


def _merge_kernel(x_ref, gt_ref, y0_ref, y1_ref, y2_ref, gates_ref, wb_ref, wo_ref, o_ref, *, d, bw):
    merged = None
    for i, y_ref in enumerate((y0_ref, y1_ref, y2_ref)):
        t = jax.nn.sigmoid(gates_ref[0, :, i * d:(i + 1) * d]) * _dot(y_ref[0], wb_ref[i * bw:(i + 1) * bw, :])
        merged = t if merged is None else merged + t
    o_ref[0] = x_ref[0] + gt_ref[0] * _dot(merged, wo_ref[...])


def _merge(x, gate, y_s5, y_rw, y_sb, gates, w_branch, w_out):
    G, R, D = x.shape
    bw = y_s5.shape[-1]
    tm = _row_tile(R, 512)
    row = lambda wd: pl.BlockSpec((1, tm, wd), lambda gi, i: (gi, i, 0))
    return pl.pallas_call(
        functools.partial(_merge_kernel, d=D, bw=bw),
        grid=(G, R // tm),
        in_specs=[row(D), _mod_spec(gate, tm), row(bw), row(bw), row(bw), row(3 * D),
                  _const_spec(w_branch.shape), _const_spec(w_out.shape)],
        out_specs=row(D),
        out_shape=jax.ShapeDtypeStruct((G, R, D), F32),
        compiler_params=_params(("parallel", "parallel")),
        name="merge",
    )(x, gate, y_s5, y_rw, y_sb, gates, w_branch, w_out)


S5_POWERS = SUBLANES


def _s5_param_kernel(lr_ref, li_ref, ldt_ref, br_ref, bi_ref, bbr_ref, bbi_ref, pwr_ref, pwi_ref):
    lr, li = lr_ref[...], li_ref[...]
    dt = jnp.exp(ldt_ref[...])
    mag = jnp.exp(lr * dt)
    ab_re, ab_im = mag * jnp.cos(li * dt), mag * jnp.sin(li * dt)
    den = lr * lr + li * li
    nr, ni = ab_re - 1.0, ab_im
    f_re = (nr * lr + ni * li) / den
    f_im = (ni * lr - nr * li) / den
    br, bi = br_ref[...], bi_ref[...]
    bbr_ref[...] = f_re * br - f_im * bi
    bbi_ref[...] = f_re * bi + f_im * br
    pr, pi = ab_re, ab_im
    for j in range(S5_POWERS):
        pwr_ref[j] = pr
        pwi_ref[j] = pi
        pr, pi = pr * ab_re - pi * ab_im, pr * ab_im + pi * ab_re


def _s5_params(lam_re, lam_im, log_dt, b_re, b_im):
    G, P = lam_re.shape
    C = b_re.shape[-1]
    sds = jax.ShapeDtypeStruct
    return pl.pallas_call(
        _s5_param_kernel,
        out_shape=[sds((G, C, P), F32), sds((G, C, P), F32),
                   sds((S5_POWERS, G, 1, P), F32), sds((S5_POWERS, G, 1, P), F32)],
        name="s5_params",
    )(lam_re.reshape(G, 1, P), lam_im.reshape(G, 1, P),
      jnp.broadcast_to(log_dt[:, None, None], (G, 1, P)),
      jnp.swapaxes(b_re, 1, 2), jnp.swapaxes(b_im, 1, 2))


S5_LANE_CHUNK = 1024


def _s5_kernel(u_ref, x0r_ref, x0i_ref, bbr_ref, bbi_ref, cr_ref, ci_ref, d_ref, wg_ref, bg_ref,
               pwr_ref, pwi_ref, y_ref, xr_out, xi_out, sr_ref, si_ref, car_ref, cai_ref,
               *, tl, nl, last_row):
    li = pl.program_id(1)

    @pl.when(li == 0)
    def _():
        car_ref[...] = x0r_ref[0]
        cai_ref[...] = x0i_ref[0]

    u = u_ref[0]
    ub = u.astype(BF16)
    n_tiles, wi, ws = bbr_ref.shape
    for g in range(n_tiles):
        ug = ub[:, g * wi:(g + 1) * wi]
        sr_ref[:, g * ws:(g + 1) * ws] = jnp.dot(ug, bbr_ref[g], preferred_element_type=F32)
        si_ref[:, g * ws:(g + 1) * ws] = jnp.dot(ug, bbi_ref[g], preferred_element_type=F32)
    n_state = sr_ref.shape[1]
    lc = S5_LANE_CHUNK
    row = lax.broadcasted_iota(jnp.int32, (SUBLANES, lc), 0)
    for c in range(n_state // lc):
        ls = slice(c * lc, (c + 1) * lc)
        pr8, pi8 = pwr_ref[:, ls], pwi_ref[:, ls]
        steps = []
        for dd in (1, 2, 4):
            steps.append((dd, jnp.where(row >= dd, pr8[dd - 1:dd, :], 0.0),
                          jnp.where(row >= dd, pi8[dd - 1:dd, :], 0.0)))

        def body(i, carry, ls=ls, pr8=pr8, pi8=pi8, steps=steps):
            car, cai = carry
            r0 = pl.multiple_of(i * SUBLANES, SUBLANES)
            xr = sr_ref[pl.ds(r0, SUBLANES), ls]
            xi = si_ref[pl.ds(r0, SUBLANES), ls]
            for dd, ar, ai in steps:
                rr = pltpu.roll(xr, dd, 0)
                ri = pltpu.roll(xi, dd, 0)
                xr, xi = xr + ar * rr - ai * ri, xi + ar * ri + ai * rr
            xr, xi = xr + pr8 * car - pi8 * cai, xi + pr8 * cai + pi8 * car
            sr_ref[pl.ds(r0, SUBLANES), ls] = xr
            si_ref[pl.ds(r0, SUBLANES), ls] = xi
            return xr[SUBLANES - 1:SUBLANES, :], xi[SUBLANES - 1:SUBLANES, :]

        car, cai = lax.fori_loop(0, tl // SUBLANES, body, (car_ref[:, ls], cai_ref[:, ls]))
        car_ref[:, ls] = car
        cai_ref[:, ls] = cai

    y = jnp.concatenate([_dot(sr_ref[:, g * ws:(g + 1) * ws], cr_ref[g]) - _dot(si_ref[:, g * ws:(g + 1) * ws], ci_ref[g])
                         for g in range(n_tiles)], axis=1) + d_ref[...] * u
    z = jax.nn.gelu(y)
    y_ref[0] = z * jax.nn.sigmoid(_dot(z, wg_ref[...]) + bg_ref[...])

    @pl.when(li == nl - 1)
    def _():
        xr_out[0] = sr_ref[last_row:last_row + 1, :]
        xi_out[0] = si_ref[last_row:last_row + 1, :]


def _s5(u, x0_re, x0_im, prm, seq_len):
    B, Lp, W = u.shape
    n_state = prm["bb_re"].shape[0] * prm["bb_re"].shape[2]
    tl = _row_tile(Lp, 256)
    nl = Lp // tl
    last_row = (seq_len - 1) - (nl - 1) * tl
    assert 0 <= last_row < tl
    state = pl.BlockSpec((1, 1, n_state), lambda b, i: (b, 0, 0))
    sds = jax.ShapeDtypeStruct
    return pl.pallas_call(
        functools.partial(_s5_kernel, tl=tl, nl=nl, last_row=last_row),
        grid=(B, nl),
        in_specs=[pl.BlockSpec((1, tl, W), lambda b, i: (b, i, 0)), state, state,
                  _const_spec(prm["bb_re"].shape), _const_spec(prm["bb_im"].shape),
                  _const_spec(prm["c_re"].shape), _const_spec(prm["c_im"].shape),
                  _const_spec((1, W)), _const_spec((W, W)), _const_spec((1, W)),
                  _const_spec((S5_POWERS, n_state)), _const_spec((S5_POWERS, n_state))],
        out_specs=[pl.BlockSpec((1, tl, W), lambda b, i: (b, i, 0)), state, state],
        out_shape=[sds((B, Lp, W), F32), sds((B, 1, n_state), F32), sds((B, 1, n_state), F32)],
        scratch_shapes=[pltpu.VMEM((tl, n_state), F32), pltpu.VMEM((tl, n_state), F32),
                        pltpu.VMEM((1, n_state), F32), pltpu.VMEM((1, n_state), F32)],
        compiler_params=_params(("parallel", "arbitrary")),
        name="s5",
    )(u, x0_re, x0_im, prm["bb_re"], prm["bb_im"], prm["c_re"], prm["c_im"], prm["d"],
      prm["w_glu"], prm["b_glu"], prm["pw_re"], prm["pw_im"])


RW_SEQS_PER_STEP = 2


def _rwkv_kernel(p_ref, sh0_ref, s0_ref, mu_ref, w0_ref, w2_ref, a0_ref, a2_ref, g2_ref, kk_ref, ka_ref,
                 rk_ref, gnw_ref, gnb_ref, seg_ref, tri_ref, y_ref, sfin_ref, S_ref, prev_ref,
                 *, T, nl, valid, heads, nb):
    li = pl.program_id(1)
    W = heads * HEAD_DIM

    @pl.when(li == 0)
    def _():
        S_ref[...] = s0_ref[...]
        prev_ref[...] = sh0_ref[...]

    trow = lax.broadcasted_iota(jnp.int32, (T, 1), 0)
    seg = seg_ref[...]

    def prep(b):
        p = p_ref[b]
        p_prev = jnp.where(trow == 0, prev_ref[b], pltpu.roll(p, 1, 0))
        prev_ref[b] = p[T - 1:T, :]
        ps = p + (p_prev - p) * mu_ref[...]
        r, k, v = ps[:, 0:W], ps[:, W:2 * W], ps[:, 2 * W:3 * W]
        o3 = 3 * W
        dw = ps[:, o3:o3 + RW_DECAY_RANK]
        da = ps[:, o3 + RW_DECAY_RANK:o3 + RW_DECAY_RANK + RW_AICL_RANK]
        dg = ps[:, o3 + RW_DECAY_RANK + RW_AICL_RANK:]
        w_log = -_softplus(-(w0_ref[...] + _dot(jnp.tanh(dw), w2_ref[...]))) - 0.5
        lw = -jnp.exp(w_log)
        a = jax.nn.sigmoid(a0_ref[...] + _dot(da, a2_ref[...]))
        g = _dot(jax.nn.sigmoid(dg), g2_ref[...])
        kk = k * kk_ref[...]
        kk = kk / jnp.maximum(jnp.sqrt(_dot_hl(kk * kk, seg)), 1e-12)
        km = k * (1.0 + (a - 1.0) * ka_ref[...])
        bonus = _dot_hl(r * km * rk_ref[...], seg)
        if valid < T:
            live = trow < valid
            lw = jnp.where(live, lw, 0.0)
            kk = jnp.where(live, kk, 0.0)
            km = jnp.where(live, km, 0.0)
        cum = _dot_lh(tri_ref[...], lw)
        c_end = cum[T - 1:T, :]
        e_neg = jnp.exp(-cum)
        e_end = jnp.exp(c_end - cum)
        kka = kk * a
        return dict(A=-kk * jnp.exp(cum - lw), Bt=kka * e_neg, Kt=km * e_neg, Rt=r * jnp.exp(cum),
                    Bh=kka * e_end, Kh=km * e_end, w_end=jnp.exp(c_end), v=v, g=g, bonus=bonus)

    X_ = [prep(b) for b in range(nb)]

    ti = lax.broadcasted_iota(jnp.int32, (T, T), 0)
    tj = lax.broadcasted_iota(jnp.int32, (T, T), 1)
    strict = ti > tj
    incl = ti >= tj
    eye = jnp.where(ti == tj, 1.0, 0.0)
    chains = [(b, h) for b in range(nb) for h in range(heads)]
    C = range(len(chains))
    hs_of = [slice(h * HEAD_DIM, (h + 1) * HEAD_DIM) for _, h in chains]
    col = lambda name, c: X_[chains[c][0]][name][:, hs_of[c]]
    Ss = [S_ref[b, h] for b, h in chains]
    ARs = [jnp.concatenate([col("A", c), col("Rt", c)], axis=0) for c in C]
    BKs = [jnp.concatenate([col("Bt", c), col("Kt", c)], axis=0) for c in C]
    Gms = [_dot_nt(ARs[c], BKs[c]) for c in C]
    ARSs = [_dot_nt(ARs[c], Ss[c]) for c in C]
    Ns = [jnp.where(strict, Gm[0:T, 0:T], 0.0) for Gm in Gms]
    Ms = [jnp.where(strict, Gm[0:T, T:2 * T], 0.0) for Gm in Gms]
    Prbs = [jnp.where(incl, Gm[T:2 * T, 0:T], 0.0) for Gm in Gms]
    Prks = [jnp.where(incl, Gm[T:2 * T, T:2 * T], 0.0) for Gm in Gms]
    vhs = [col("v", c) for c in C]
    MVs = [_dot(Ms[c], vhs[c]) for c in C]
    Xs = [eye + N for N in Ns]
    Ps = Ns
    n = 1
    while 2 * n < T:
        Ps = [_dot(P, P) for P in Ps]
        Xs = [Xs[c] + _dot(Xs[c], Ps[c]) for c in C]
        n *= 2
    Us = [_dot(Xs[c], ARSs[c][0:T] + MVs[c]) for c in C]
    Os = [ARSs[c][T:2 * T] + _dot(Prbs[c], Us[c]) + _dot(Prks[c], vhs[c]) for c in C]
    for c in C:
        b, h = chains[c]
        S_ref[b, h] = Ss[c] * col("w_end", c) + _dot_tn(Us[c], col("Bh", c)) + _dot_tn(vhs[c], col("Kh", c))
    for c in C:
        b, h = chains[c]
        hs = hs_of[c]
        O = Os[c]
        mean = jnp.mean(O, axis=-1, keepdims=True)
        var = jnp.mean(jnp.square(O - mean), axis=-1, keepdims=True)
        on = (O - mean) * lax.rsqrt(var + RW_GN_EPS) * gnw_ref[:, hs] + gnb_ref[:, hs]
        on = on + col("bonus", c) * vhs[c]
        y_ref[b, :, hs] = on * col("g", c)

    @pl.when(li == nl - 1)
    def _():
        sfin_ref[...] = S_ref[...]


def _rwkv(p, shift0, s0, prm, seq_len, T):
    B, Lp, NP = p.shape
    heads = s0.shape[1]
    W = heads * HEAD_DIM
    nl = Lp // T
    valid = seq_len - (nl - 1) * T
    assert nl * T == Lp and 0 < valid <= T and (nl == 1 or valid == T)
    nb = RW_SEQS_PER_STEP if B % RW_SEQS_PER_STEP == 0 else 1
    tri = jnp.tril(jnp.ones((T, T), BF16))
    vec = lambda n: _const_spec((1, n))
    sds = jax.ShapeDtypeStruct
    st_spec = pl.BlockSpec((nb, heads, HEAD_DIM, HEAD_DIM), lambda b, i: (b, 0, 0, 0))
    return pl.pallas_call(
        functools.partial(_rwkv_kernel, T=T, nl=nl, valid=valid, heads=heads, nb=nb),
        grid=(B // nb, nl),
        in_specs=[pl.BlockSpec((nb, T, NP), lambda b, i: (b, i, 0)),
                  pl.BlockSpec((nb, 1, NP), lambda b, i: (b, 0, 0)), st_spec,
                  vec(NP), vec(W), _const_spec((RW_DECAY_RANK, W)), vec(W), _const_spec((RW_AICL_RANK, W)),
                  _const_spec((RW_GATE_RANK, W)), vec(W), vec(W), vec(W), vec(W), vec(W),
                  _const_spec((W, W)), _const_spec((T, T))],
        out_specs=[pl.BlockSpec((nb, T, W), lambda b, i: (b, i, 0)), st_spec],
        out_shape=[sds((B, Lp, W), F32), sds(s0.shape, F32)],
        scratch_shapes=[pltpu.VMEM((nb, heads, HEAD_DIM, HEAD_DIM), F32), pltpu.VMEM((nb, 1, NP), F32)],
        compiler_params=_params(("parallel", "arbitrary")),
        name="rwkv",
    )(p, shift0, s0, prm["mu"], prm["w0"], prm["w2"], prm["a0"], prm["a2"], prm["g2"], prm["k_k"],
      prm["k_a"], prm["r_k"], prm["gn_w"], prm["gn_b"], prm["seg"], tri)


SB_TILE = MXU_WIDTH
SB_Q_SUB = 2
SB_CAST_ROWS = 512
SB_BIAS_SPLIT = 3
LOG2E = math.log2(math.e)


def _sb_tiles(qs, kvs, carries, masks, upper):
    C, J = range(len(qs)), range(len(kvs))
    nzs = [[_dot_nt(qs[c], kvs[j][c][0]) for c in C] for j in J]
    ms = [[jnp.minimum(nzs[j][c], 0.0) for c in C] for j in J]
    ps = [[ms[j][c] - nzs[j][c] for c in C] for j in J]
    ls = [[jnp.log2(1.0 + jnp.exp2(ms[j][c] + ps[j][c])) for c in C] for j in J]
    lks = [[ms[j][c] - ls[j][c] for c in C] for j in J]
    lks = [[lks[j][c] if masks[j][c] is None else jnp.where(masks[j][c], lks[j][c], 0.0) for c in C] for j in J]
    cin = [list(carries)]
    for j in J:
        cin.append([cin[j][c] + jnp.sum(lks[j][c], axis=-1, keepdims=True) for c in C])
    css = [[_dot(lks[j][c], upper) + cin[j][c] for c in C] for j in J]
    ws = [[jnp.exp2(ps[j][c] - ls[j][c] + css[j][c]) for c in C] for j in J]
    ws = [[ws[j][c] if masks[j][c] is None else jnp.where(masks[j][c], ws[j][c], 0.0) for c in C] for j in J]
    ds = [[_dot(ws[j][c], kvs[j][c][1]) for c in C] for j in J]
    out = ds[0]
    for j in J[1:]:
        out = [out[c] + ds[j][c] for c in C]
    return out, cin[-1]


SB_K_PER_ITER = 2


def _sb_kernel(q_ref, k_ref, v_ref, bias_ref, up_ref, o_ref, kb_ref, vb_ref, *, t, nsub, scale, pair, L):
    qi = pl.program_id(2)
    heads = range(pair)
    lane = lax.broadcasted_iota(jnp.int32, (1, LANES), 1)
    is_data = lane < HEAD_DIM

    def head_lanes(x, hh):
        return x if hh == 0 else pltpu.roll(x, LANES - hh * HEAD_DIM, 1)

    @pl.when(qi == 0)
    def _():
        rows = min(SB_CAST_ROWS, L)
        ones = jnp.where(lane < HEAD_DIM + SB_BIAS_SPLIT, 1.0, 0.0)

        def cast(c, _):
            r0 = pl.multiple_of(c * rows, rows)
            kk = k_ref[0, pl.ds(r0, rows), :]
            vv = v_ref[0, pl.ds(r0, rows), :]
            for hh in heads:
                kb_ref[hh, pl.ds(r0, rows), :] = jnp.where(is_data, head_lanes(kk, hh), ones).astype(BF16)
                vb_ref[hh, pl.ds(r0, rows), :] = head_lanes(vv, hh)[:, 0:HEAD_DIM].astype(BF16)
            return 0

        lax.fori_loop(0, L // rows, cast, 0)

    q2 = q_ref[0] * (-scale * LOG2E)
    qs = [jnp.where(is_data, head_lanes(q2[s * t:(s + 1) * t, :], hh), bias_ref[0, hh:hh + 1, :]).astype(BF16)
          for s in range(nsub) for hh in heads]
    upper = up_ref[...]
    ri = lax.broadcasted_iota(jnp.int32, (t, t), 0)
    ci = lax.broadcasted_iota(jnp.int32, (t, t), 1)
    diag_mask = ci < ri

    def kv_tile(k0, chains):
        return [(kb_ref[c % pair, pl.ds(k0, t), :], vb_ref[c % pair, pl.ds(k0, t), :]) for c in chains]

    zero = jnp.zeros((t, 1), F32)
    accs = [None] * (nsub * pair)
    carries = [zero] * (nsub * pair)
    for kt in reversed(range(nsub)):
        chains = list(range(kt * pair, nsub * pair))
        k0 = pl.multiple_of((qi * nsub + kt) * t, t)
        ds, cs = _sb_tiles([qs[c] for c in chains], [kv_tile(k0, chains)], [carries[c] for c in chains],
                           [[diag_mask if c // pair == kt else None for c in chains]], upper)
        for i, c in enumerate(chains):
            accs[c] = ds[i] if accs[c] is None else accs[c] + ds[i]
            carries[c] = cs[i]

    chains = list(range(nsub * pair))
    n_tiles = qi * nsub
    kpi = SB_K_PER_ITER if nsub % SB_K_PER_ITER == 0 else 1
    no_mask = [[None] * len(chains)] * kpi

    def body(it, st):
        accs, carries = st
        k0s = [pl.multiple_of((n_tiles - 1 - it * kpi - j) * t, t) for j in range(kpi)]
        ds, carries = _sb_tiles(qs, [kv_tile(k0, chains) for k0 in k0s], list(carries), no_mask, upper)
        return tuple(a + d for a, d in zip(accs, ds)), tuple(carries)

    accs, _ = lax.fori_loop(0, n_tiles // kpi, body, (tuple(accs), tuple(carries)))
    for c in chains:
        s, hh = divmod(c, pair)
        o_ref[0, s * t:(s + 1) * t, hh * HEAD_DIM:(hh + 1) * HEAD_DIM] = accs[c]


def _sb_prompt(q, k, v, bias):
    B, L, W = q.shape
    H = W // HEAD_DIM
    pair = LANES // HEAD_DIM
    t = _row_tile(L, SB_TILE)
    nsub = SB_Q_SUB if L % (SB_Q_SUB * t) == 0 else 1
    upper = jnp.triu(jnp.ones((t, t), BF16), 1).T
    rest = -bias * LOG2E
    parts = []
    for _ in range(SB_BIAS_SPLIT):
        parts.append(rest.astype(BF16).astype(F32))
        rest = rest - parts[-1]
    bias_rows = jnp.zeros((H, LANES), F32).at[:, HEAD_DIM:HEAD_DIM + SB_BIAS_SPLIT].set(jnp.stack(parts, axis=1))
    qspec = pl.BlockSpec((1, nsub * t, LANES), lambda b, hp, i: (b, i, hp))
    kvspec = pl.BlockSpec((1, L, LANES), lambda b, hp, i: (b, 0, hp))
    return pl.pallas_call(
        functools.partial(_sb_kernel, t=t, nsub=nsub, scale=HEAD_DIM ** -0.5, pair=pair, L=L),
        grid=(B, W // LANES, L // (nsub * t)),
        in_specs=[qspec, kvspec, kvspec, pl.BlockSpec((1, pair, LANES), lambda b, hp, i: (hp, 0, 0)),
                  _const_spec((t, t))],
        out_specs=qspec,
        out_shape=jax.ShapeDtypeStruct((B, L, W), F32),
        scratch_shapes=[pltpu.VMEM((pair, L, LANES), BF16), pltpu.VMEM((pair, L, HEAD_DIM), BF16)],
        compiler_params=_params(("parallel", "parallel", "arbitrary")),
        name="sb_prompt",
    )(q, k, v, bias_rows.reshape(H // pair, pair, LANES), upper)


SB_DEC_PAGES = 8


def _sb_decode_kernel(pt_ref, q_ref, nbias_ref, *refs, pps, scale):
    k_refs, v_refs = refs[:pps], refs[pps:2 * pps]
    o_ref, acc_ref, carry_ref = refs[2 * pps:]
    step = pl.program_id(1)
    H, d, page = acc_ref.shape

    @pl.when(step == 0)
    def _():
        acc_ref[...] = jnp.zeros_like(acc_ref)
        carry_ref[...] = jnp.zeros_like(carry_ref)

    qb = jnp.broadcast_to(q_ref[0] * (-scale * LOG2E), (H, d, page))
    lane = lax.broadcasted_iota(jnp.int32, (H, 1, page), 2)
    pages = range(pps)
    nzs = [jnp.sum(k_refs[i][0, 0] * qb, axis=1, keepdims=True) + nbias_ref[...] for i in pages]
    lks = [jnp.minimum(nz, 0.0) - jnp.log2(1.0 + jnp.exp2(-jnp.abs(nz))) for nz in nzs]
    sufs = list(lks)
    sh = 1
    while sh < page:
        sufs = [x + jnp.where(lane < page - sh, pltpu.roll(x, page - sh, 2), 0.0) for x in sufs]
        sh *= 2
    carry = carry_ref[...]
    css = []
    for i in pages:
        css.append(sufs[i] - lks[i] + carry)
        carry = carry + sufs[i][:, :, 0:1]
    carry_ref[...] = carry
    ws = [jnp.exp2(lks[i] - nzs[i] + css[i]) for i in pages]
    acc = acc_ref[...]
    for i in pages:
        acc = acc + v_refs[i][0, 0] * ws[i]
    acc_ref[...] = acc

    @pl.when(step == pl.num_programs(1) - 1)
    def _():
        o_ref[0] = jnp.sum(acc, axis=-1, keepdims=True)


def _sb_decode(q, cache_k, cache_v, layer, page_table, bias):
    B, _, W = q.shape
    n_pages = page_table.shape[1]
    depth, n_pool, page, H, d = cache_k.shape
    pps = SB_DEC_PAGES if n_pages % SB_DEC_PAGES == 0 else 1
    ck = jnp.transpose(cache_k, (0, 1, 3, 4, 2))
    cv = jnp.transpose(cache_v, (0, 1, 3, 4, 2))
    nbias = jnp.broadcast_to((-bias * LOG2E)[:, None, None], (H, 1, page))
    kv = [pl.BlockSpec((1, 1, H, d, page),
                       lambda b, s, pt, i=i: (layer, pt[b, n_pages - 1 - (s * pps + i)], 0, 0, 0))
          for i in range(pps)]
    qo = pl.BlockSpec((1, H, d, 1), lambda b, s, pt: (b, 0, 0, 0))
    out = pl.pallas_call(
        functools.partial(_sb_decode_kernel, pps=pps, scale=d ** -0.5),
        grid_spec=pltpu.PrefetchScalarGridSpec(
            num_scalar_prefetch=1,
            grid=(B, n_pages // pps),
            in_specs=[qo, pl.BlockSpec((H, 1, page), lambda b, s, pt: (0, 0, 0))] + kv + kv,
            out_specs=qo,
            scratch_shapes=[pltpu.VMEM((H, d, page), F32), pltpu.VMEM((H, 1, page), F32)]),
        out_shape=jax.ShapeDtypeStruct((B, H, d, 1), F32),
        compiler_params=_params(("parallel", "arbitrary")),
        name="sb_decode",
    )(page_table, q.reshape(B, H, d, 1), nbias, *([ck] * pps), *([cv] * pps))
    return out.reshape(B, 1, W)


def _block_diag(m):
    G, a, b = m.shape
    per = LANES // min(a, b)
    eye = jnp.eye(per, dtype=m.dtype)
    mt = m.reshape(G // per, per, a, b)
    return (mt[:, :, :, None, :] * eye[None, :, None, :, None]).reshape(G // per, per * a, per * b)


def _layer(x, mod, P, seq_len, s5_0, rw_s0, rw_shift0, sb_fn, norm_f):
    G, R, D = x.shape
    m = lambda i, j: mod[:, :, i, j, :]
    x = _ffn(x, m(0, 0), m(0, 1), m(0, 2), P["norm_g"][0], P["w_up"][0], P["w_down"][0])
    u, p, q, k, v, gates = _inproj(x, m(1, 0), m(1, 1), P["norm_g"][1], P["w_in"], P["widths"])
    B = G * R // seq_len
    seqs = lambda t: t.reshape(B, seq_len, t.shape[-1])
    pad = (-seq_len) % SUBLANES

    def padded(t):
        t = seqs(t)
        return jnp.pad(t, ((0, 0), (0, pad), (0, 0))) if pad else t

    y_s5, s5_re, s5_im = _s5(padded(u), s5_0[0], s5_0[1], P["s5"], seq_len)
    T = 64 if seq_len % 64 == 0 else SUBLANES
    y_rw, rw_s = _rwkv(padded(p), rw_shift0, rw_s0, P["rw"], seq_len, T)
    y_sb = sb_fn(seqs(q), seqs(k), seqs(v))
    rows = lambda t: t[:, :seq_len].reshape(G, R, t.shape[-1])
    x = _merge(x, m(1, 2), rows(y_s5), rows(y_rw), rows(y_sb), gates, P["w_branch"], P["w_out"])
    x = _ffn(x, m(2, 0), m(2, 1), m(2, 2), P["norm_g"][2], P["w_up"][1], P["w_down"][1], norm_f)
    return x, (seqs(k), seqs(v), s5_re, s5_im, rw_s, seqs(p)[:, -1])


def kernel(x_prompt, x_sample, cache_k, cache_v, state_s5_re, state_s5_im, state_rwkv, state_rwkv_shift, page_table, c_prompt, c_sample, norm_g, w_ada, b_ada, w_ffn_up, w_ffn_down, w_in, s5_lam_re, s5_lam_im, s5_log_dt, s5_b_re, s5_b_im, s5_c_re, s5_c_im, s5_d, s5_w_glu, s5_b_glu, rw_mu, rw_w0, rw_w2, rw_a0, rw_a2, rw_g2, rw_k_k, rw_k_a, rw_r_k, rw_gn_w, rw_gn_b, sb_bias, w_branch, w_out, norm_f):
    bp, L, D = x_prompt.shape
    bs = x_sample.shape[0]
    depth = w_in.shape[0]
    G5, P5 = s5_lam_re.shape[1:]
    n_state = G5 * P5
    s5_w = G5 * S5_GROUP
    heads = state_rwkv.shape[2]
    W = heads * HEAD_DIM
    n_proj = state_rwkv_shift.shape[-1]
    widths = (s5_w, n_proj, W, W, W, 3 * D)

    c_all = jnp.concatenate([c_prompt, c_sample], axis=0)
    c_all = jnp.pad(c_all, ((0, (-c_all.shape[0]) % SUBLANES), (0, 0)))
    mod_all = _ada(c_all, w_ada, b_ada)
    mod_p = mod_all[:, :bp].reshape(depth, bp, 1, 3, 3, D)
    mod_s = mod_all[:, bp:bp + bs].reshape(depth, 1, bs, 3, 3, D)

    seg = (jnp.arange(W)[:, None] // HEAD_DIM == jnp.arange(W)[None, :] // HEAD_DIM).astype(BF16)
    xp = x_prompt
    xs = x_sample.reshape(1, bs, D)
    outs_p, outs_s = [], []
    for l in range(depth):
        bb_re, bb_im, pw_re, pw_im = _s5_params(s5_lam_re[l], s5_lam_im[l], s5_log_dt[l], s5_b_re[l], s5_b_im[l])
        P = dict(
            norm_g=norm_g[l], w_up=w_ffn_up[l].astype(BF16), w_down=w_ffn_down[l].astype(BF16),
            w_in=w_in[l].astype(BF16), widths=widths,
            w_branch=w_branch[l].astype(BF16), w_out=w_out[l].astype(BF16),
            s5=dict(bb_re=_block_diag(bb_re).astype(BF16), bb_im=_block_diag(bb_im).astype(BF16),
                    c_re=_block_diag(jnp.swapaxes(s5_c_re[l], 1, 2)).astype(BF16),
                    c_im=_block_diag(jnp.swapaxes(s5_c_im[l], 1, 2)).astype(BF16),
                    d=s5_d[l].reshape(1, s5_w), w_glu=s5_w_glu[l].astype(BF16), b_glu=s5_b_glu[l].reshape(1, s5_w),
                    pw_re=pw_re.reshape(S5_POWERS, n_state), pw_im=pw_im.reshape(S5_POWERS, n_state)),
            rw=dict(mu=rw_mu[l].reshape(1, n_proj), w0=rw_w0[l].reshape(1, W), w2=rw_w2[l].astype(BF16),
                    a0=rw_a0[l].reshape(1, W), a2=rw_a2[l].astype(BF16), g2=rw_g2[l].astype(BF16),
                    k_k=rw_k_k[l].reshape(1, W), k_a=rw_k_a[l].reshape(1, W), r_k=rw_r_k[l].reshape(1, W),
                    gn_w=rw_gn_w[l].reshape(1, W), gn_b=rw_gn_b[l].reshape(1, W), seg=seg))
        nf = norm_f if l == depth - 1 else None
        bias = sb_bias[l]
        zs5 = jnp.zeros((bp, 1, n_state), F32)
        xp, st = _layer(xp, mod_p[l], P, L, (zs5, zs5), jnp.zeros((bp, heads, HEAD_DIM, HEAD_DIM), F32),
                        jnp.zeros((bp, 1, n_proj), F32),
                        lambda q, k, v: _sb_prompt(q, k, v, bias), nf)
        outs_p.append(st)
        xs, st = _layer(xs, mod_s[l], P, 1,
                        (state_s5_re[l].reshape(bs, 1, n_state), state_s5_im[l].reshape(bs, 1, n_state)),
                        state_rwkv[l], state_rwkv_shift[l].reshape(bs, 1, n_proj),
                        lambda q, k, v: _sb_decode(q, cache_k, cache_v, l, page_table, bias), nf)
        outs_s.append(st)

    def stacked(outs, i, shape):
        return jnp.stack([o[i] for o in outs]).reshape((depth,) + shape)

    def group(outs, b, seq):
        return (stacked(outs, 0, (b, seq, heads, HEAD_DIM)), stacked(outs, 1, (b, seq, heads, HEAD_DIM)),
                stacked(outs, 2, (b, G5, P5)), stacked(outs, 3, (b, G5, P5)),
                stacked(outs, 4, (b, heads, HEAD_DIM, HEAD_DIM)), stacked(outs, 5, (b, n_proj)))

    return (xp, xs.reshape(bs, 1, D)) + group(outs_p, bp, L) + group(outs_s, bs, 1)
```

```python
import functools
import math

import jax
import jax.numpy as jnp
from jax import lax
from jax.experimental import pallas as pl
from jax.experimental.pallas import tpu as pltpu

F32 = jnp.float32
BF16 = jnp.bfloat16

S5_GROUP = 16
S5_STATE = 64
HEAD_DIM = 64
RW_DECAY_RANK = 64
RW_AICL_RANK = 64
RW_GATE_RANK = 128
RW_GN_EPS = 64e-5
HALF_STEP = 0.5
RMS_EPS = 1e-6
PAGE_SIZE = 128

LANES = 128
SUBLANES = 8
MXU_WIDTH = 256
VMEM_LIMIT = 52 * 1024 * 1024


def _params(sem):
    return pltpu.CompilerParams(dimension_semantics=sem, vmem_limit_bytes=VMEM_LIMIT)


def _const_spec(shape):
    nd = len(shape)
    return pl.BlockSpec(shape, lambda *_: (0,) * nd, pipeline_mode=pl.Buffered(1))


def _dot(a, b):
    return jnp.dot(a.astype(BF16), b.astype(BF16), preferred_element_type=F32)


def _dot_nt(a, b):
    return lax.dot_general(a.astype(BF16), b.astype(BF16), (((1,), (1,)), ((), ())),
                           preferred_element_type=F32)


def _dot_tn(a, b):
    return lax.dot_general(a.astype(BF16), b.astype(BF16), (((0,), (0,)), ((), ())),
                           preferred_element_type=F32)


def _split(a):
    hi = a.astype(BF16)
    lo = (a - hi.astype(F32)).astype(BF16)
    return hi, lo


def _dot_hl(a, b01):
    hi, lo = _split(a)
    return (jnp.dot(hi, b01, preferred_element_type=F32)
            + jnp.dot(lo, b01, preferred_element_type=F32))


def _dot_lh(a01, b):
    hi, lo = _split(b)
    return (jnp.dot(a01, hi, preferred_element_type=F32)
            + jnp.dot(a01, lo, preferred_element_type=F32))


def _softplus(x):
    return jnp.maximum(x, 0.0) + jnp.log1p(jnp.exp(-jnp.abs(x)))


def _norm_mod(x, g, scale, shift):
    ms = jnp.mean(x * x, axis=-1, keepdims=True)
    return (x * lax.rsqrt(ms + RMS_EPS) * g) * (1.0 + scale) + shift


def _row_tile(rows, want):
    return want if rows % want == 0 else rows


def _mod_spec(mod, tm):
    _, rm, d = mod.shape
    if rm == 1:
        return pl.BlockSpec((1, 1, d), lambda g, i: (g, 0, 0))
    return pl.BlockSpec((1, tm, d), lambda g, i: (g, i, 0))


def _ada_kernel(c_ref, w_ref, b_ref, o_ref):
    c = c_ref[...]
    o_ref[0] = _dot(c * jax.nn.sigmoid(c), w_ref[0]) + b_ref[0]


def _ada(c_all, w_ada, b_ada):
    depth, d, n = w_ada.shape
    rows = c_all.shape[0]
    tn = 1024
    return pl.pallas_call(
        _ada_kernel,
        grid=(depth, n // tn),
        in_specs=[pl.BlockSpec((rows, d), lambda l, j: (0, 0)),
                  pl.BlockSpec((1, d, tn), lambda l, j: (l, 0, j)),
                  pl.BlockSpec((1, 1, tn), lambda l, j: (l, 0, j))],
        out_specs=pl.BlockSpec((1, rows, tn), lambda l, j: (l, 0, j)),
        out_shape=jax.ShapeDtypeStruct((depth, rows, n), F32),
        compiler_params=_params(("parallel", "parallel")),
        name="adaln",
    )(c_all, w_ada, b_ada.reshape(depth, 1, n))


def _ffn_kernel(x_ref, sh_ref, sc_ref, gt_ref, g_ref, wup_ref, wdn_ref, *rest, dff, chunk, final):
    if final:
        nf_ref, o_ref, acc_ref = rest
    else:
        o_ref, acc_ref = rest
    x = x_ref[0]
    h = _norm_mod(x, g_ref[...], sc_ref[0], sh_ref[0]).astype(BF16)
    for j in range(dff // chunk):
        c0, c1 = j * chunk, (j + 1) * chunk
        g = jnp.dot(h, wup_ref[:, c0:c1], preferred_element_type=F32)
        u = jnp.dot(h, wup_ref[:, dff + c0:dff + c1], preferred_element_type=F32)
        act = (g * jax.nn.sigmoid(g) * u).astype(BF16)
        d = jnp.dot(act, wdn_ref[c0:c1, :], preferred_element_type=F32)
        if j == 0:
            acc_ref[...] = d
        else:
            acc_ref[...] += d
    y = x + HALF_STEP * gt_ref[0] * acc_ref[...]
    if final:
        ms = jnp.mean(y * y, axis=-1, keepdims=True)
        y = y * lax.rsqrt(ms + RMS_EPS) * nf_ref[...]
    o_ref[0] = y


def _ffn(x, shift, scale, gate, g, w_up, w_down, norm_f=None):
    G, R, D = x.shape
    dff = w_down.shape[0]
    tm = _row_tile(R, 512)
    final = norm_f is not None
    in_specs = [pl.BlockSpec((1, tm, D), lambda gi, i: (gi, i, 0)),
                _mod_spec(shift, tm), _mod_spec(scale, tm), _mod_spec(gate, tm),
                _const_spec((1, D)), _const_spec(w_up.shape), _const_spec(w_down.shape)]
    args = [x, shift, scale, gate, g.reshape(1, D), w_up, w_down]
    if final:
        in_specs.append(_const_spec((1, D)))
        args.append(norm_f.reshape(1, D))
    return pl.pallas_call(
        functools.partial(_ffn_kernel, dff=dff, chunk=MXU_WIDTH, final=final),
        grid=(G, R // tm),
        in_specs=in_specs,
        out_specs=pl.BlockSpec((1, tm, D), lambda gi, i: (gi, i, 0)),
        out_shape=jax.ShapeDtypeStruct((G, R, D), F32),
        scratch_shapes=[pltpu.VMEM((tm, D), F32)],
        compiler_params=_params(("parallel", "parallel")),
        name="ffn",
    )(*args)


def _inproj_kernel(x_ref, sh_ref, sc_ref, g_ref, w_ref, *o_refs, widths, chunk):
    h = _norm_mod(x_ref[0], g_ref[...], sc_ref[0], sh_ref[0]).astype(BF16)
    off = 0
    for o_ref, wd in zip(o_refs, widths):
        for c0 in range(0, wd, chunk):
            o_ref[0, :, c0:c0 + chunk] = jnp.dot(h, w_ref[:, off + c0:off + c0 + chunk],
                                                 preferred_element_type=F32)
        off += wd


def _inproj(x, shift, scale, g, w_in, widths):
    G, R, D = x.shape
    tm = _row_tile(R, 256)
    return pl.pallas_call(
        functools.partial(_inproj_kernel, widths=widths, chunk=MXU_WIDTH),
        grid=(G, R // tm),
        in_specs=[pl.BlockSpec((1, tm, D), lambda gi, i: (gi, i, 0)),
                  _mod_spec(shift, tm), _mod_spec(scale, tm),
                  _const_spec((1, D)), _const_spec(w_in.shape)],
        out_specs=[pl.BlockSpec((1, tm, wd), lambda gi, i: (gi, i, 0)) for wd in widths],
        out_shape=[jax.ShapeDtypeStruct((G, R, wd), F32) for wd in widths],
        compiler_params=_params(("parallel", "parallel")),
        name="inproj",
    )(x, shift, scale, g.reshape(1, D), w_in)


def _merge_kernel(x_ref, gt_ref, y0_ref, y1_ref, y2_ref, gates_ref, wb_ref, wo_ref, o_ref, *, d, bw):
    merged = None
    for i, y_ref in enumerate((y0_ref, y1_ref, y2_ref)):
        t = jax.nn.sigmoid(gates_ref[0, :, i * d:(i + 1) * d]) * _dot(y_ref[0], wb_ref[i * bw:(i + 1) * bw, :])
        merged = t if merged is None else merged + t
    o_ref[0] = x_ref[0] + gt_ref[0] * _dot(merged, wo_ref[...])


def _merge(x, gate, y_s5, y_rw, y_sb, gates, w_branch, w_out):
    G, R, D = x.shape
    bw = y_s5.shape[-1]
    tm = _row_tile(R, 512)
    row = lambda wd: pl.BlockSpec((1, tm, wd), lambda gi, i: (gi, i, 0))
    return pl.pallas_call(
        functools.partial(_merge_kernel, d=D, bw=bw),
        grid=(G, R // tm),
        in_specs=[row(D), _mod_spec(gate, tm), row(bw), row(bw), row(bw), row(3 * D),
                  _const_spec(w_branch.shape), _const_spec(w_out.shape)],
        out_specs=row(D),
        out_shape=jax.ShapeDtypeStruct((G, R, D), F32),
        compiler_params=_params(("parallel", "parallel")),
        name="merge",
    )(x, gate, y_s5, y_rw, y_sb, gates, w_branch, w_out)


S5_POWERS = SUBLANES


def _s5_param_kernel(lr_ref, li_ref, ldt_ref, br_ref, bi_ref, bbr_ref, bbi_ref, pwr_ref, pwi_ref):
    lr, li = lr_ref[...], li_ref[...]
    dt = jnp.exp(ldt_ref[...])
    mag = jnp.exp(lr * dt)
    ab_re, ab_im = mag * jnp.cos(li * dt), mag * jnp.sin(li * dt)
    den = lr * lr + li * li
    nr, ni = ab_re - 1.0, ab_im
    f_re = (nr * lr + ni * li) / den
    f_im = (ni * lr - nr * li) / den
    br, bi = br_ref[...], bi_ref[...]
    bbr_ref[...] = f_re * br - f_im * bi
    bbi_ref[...] = f_re * bi + f_im * br
    pr, pi = ab_re, ab_im
    for j in range(S5_POWERS):
        pwr_ref[j] = pr
        pwi_ref[j] = pi
        pr, pi = pr * ab_re - pi * ab_im, pr * ab_im + pi * ab_re


def _s5_params(lam_re, lam_im, log_dt, b_re, b_im):
    G, P = lam_re.shape
    C = b_re.shape[-1]
    sds = jax.ShapeDtypeStruct
    return pl.pallas_call(
        _s5_param_kernel,
        out_shape=[sds((G, C, P), F32), sds((G, C, P), F32),
                   sds((S5_POWERS, G, 1, P), F32), sds((S5_POWERS, G, 1, P), F32)],
        name="s5_params",
    )(lam_re.reshape(G, 1, P), lam_im.reshape(G, 1, P),
      jnp.broadcast_to(log_dt[:, None, None], (G, 1, P)),
      jnp.swapaxes(b_re, 1, 2), jnp.swapaxes(b_im, 1, 2))


S5_LANE_CHUNK = 1024


def _s5_kernel(u_ref, x0r_ref, x0i_ref, bbr_ref, bbi_ref, cr_ref, ci_ref, d_ref, wg_ref, bg_ref,
               pwr_ref, pwi_ref, y_ref, xr_out, xi_out, sr_ref, si_ref, car_ref, cai_ref,
               *, tl, nl, last_row):
    li = pl.program_id(1)

    @pl.when(li == 0)
    def _():
        car_ref[...] = x0r_ref[0]
        cai_ref[...] = x0i_ref[0]

    u = u_ref[0]
    ub = u.astype(BF16)
    n_tiles, wi, ws = bbr_ref.shape
    for g in range(n_tiles):
        ug = ub[:, g * wi:(g + 1) * wi]
        sr_ref[:, g * ws:(g + 1) * ws] = jnp.dot(ug, bbr_ref[g], preferred_element_type=F32)
        si_ref[:, g * ws:(g + 1) * ws] = jnp.dot(ug, bbi_ref[g], preferred_element_type=F32)
    n_state = sr_ref.shape[1]
    lc = S5_LANE_CHUNK
    row = lax.broadcasted_iota(jnp.int32, (SUBLANES, lc), 0)
    for c in range(n_state // lc):
        ls = slice(c * lc, (c + 1) * lc)
        pr8, pi8 = pwr_ref[:, ls], pwi_ref[:, ls]
        steps = []
        for dd in (1, 2, 4):
            steps.append((dd, jnp.where(row >= dd, pr8[dd - 1:dd, :], 0.0),
                          jnp.where(row >= dd, pi8[dd - 1:dd, :], 0.0)))

        def body(i, carry, ls=ls, pr8=pr8, pi8=pi8, steps=steps):
            car, cai = carry
            r0 = pl.multiple_of(i * SUBLANES, SUBLANES)
            xr = sr_ref[pl.ds(r0, SUBLANES), ls]
            xi = si_ref[pl.ds(r0, SUBLANES), ls]
            for dd, ar, ai in steps:
                rr = pltpu.roll(xr, dd, 0)
                ri = pltpu.roll(xi, dd, 0)
                xr, xi = xr + ar * rr - ai * ri, xi + ar * ri + ai * rr
            xr, xi = xr + pr8 * car - pi8 * cai, xi + pr8 * cai + pi8 * car
            sr_ref[pl.ds(r0, SUBLANES), ls] = xr
            si_ref[pl.ds(r0, SUBLANES), ls] = xi
            return xr[SUBLANES - 1:SUBLANES, :], xi[SUBLANES - 1:SUBLANES, :]

        car, cai = lax.fori_loop(0, tl // SUBLANES, body, (car_ref[:, ls], cai_ref[:, ls]))
        car_ref[:, ls] = car
        cai_ref[:, ls] = cai

    y = jnp.concatenate([_dot(sr_ref[:, g * ws:(g + 1) * ws], cr_ref[g]) - _dot(si_ref[:, g * ws:(g + 1) * ws], ci_ref[g])
                         for g in range(n_tiles)], axis=1) + d_ref[...] * u
    z = jax.nn.gelu(y)
    y_ref[0] = z * jax.nn.sigmoid(_dot(z, wg_ref[...]) + bg_ref[...])

    @pl.when(li == nl - 1)
    def _():
        xr_out[0] = sr_ref[last_row:last_row + 1, :]
        xi_out[0] = si_ref[last_row:last_row + 1, :]


def _s5(u, x0_re, x0_im, prm, seq_len):
    B, Lp, W = u.shape
    n_state = prm["bb_re"].shape[0] * prm["bb_re"].shape[2]
    tl = _row_tile(Lp, 256)
    nl = Lp // tl
    last_row = (seq_len - 1) - (nl - 1) * tl
    assert 0 <= last_row < tl
    state = pl.BlockSpec((1, 1, n_state), lambda b, i: (b, 0, 0))
    sds = jax.ShapeDtypeStruct
    return pl.pallas_call(
        functools.partial(_s5_kernel, tl=tl, nl=nl, last_row=last_row),
        grid=(B, nl),
        in_specs=[pl.BlockSpec((1, tl, W), lambda b, i: (b, i, 0)), state, state,
                  _const_spec(prm["bb_re"].shape), _const_spec(prm["bb_im"].shape),
                  _const_spec(prm["c_re"].shape), _const_spec(prm["c_im"].shape),
                  _const_spec((1, W)), _const_spec((W, W)), _const_spec((1, W)),
                  _const_spec((S5_POWERS, n_state)), _const_spec((S5_POWERS, n_state))],
        out_specs=[pl.BlockSpec((1, tl, W), lambda b, i: (b, i, 0)), state, state],
        out_shape=[sds((B, Lp, W), F32), sds((B, 1, n_state), F32), sds((B, 1, n_state), F32)],
        scratch_shapes=[pltpu.VMEM((tl, n_state), F32), pltpu.VMEM((tl, n_state), F32),
                        pltpu.VMEM((1, n_state), F32), pltpu.VMEM((1, n_state), F32)],
        compiler_params=_params(("parallel", "arbitrary")),
        name="s5",
    )(u, x0_re, x0_im, prm["bb_re"], prm["bb_im"], prm["c_re"], prm["c_im"], prm["d"],
      prm["w_glu"], prm["b_glu"], prm["pw_re"], prm["pw_im"])


RW_SEQS_PER_STEP = 2


def _rwkv_kernel(p_ref, sh0_ref, s0_ref, mu_ref, w0_ref, w2_ref, a0_ref, a2_ref, g2_ref, kk_ref, ka_ref,
                 rk_ref, gnw_ref, gnb_ref, seg_ref, tri_ref, y_ref, sfin_ref, S_ref, prev_ref,
                 *, T, nl, valid, heads, nb):
    li = pl.program_id(1)
    W = heads * HEAD_DIM

    @pl.when(li == 0)
    def _():
        S_ref[...] = s0_ref[...]
        prev_ref[...] = sh0_ref[...]

    trow = lax.broadcasted_iota(jnp.int32, (T, 1), 0)
    seg = seg_ref[...]

    def prep(b):
        p = p_ref[b]
        p_prev = jnp.where(trow == 0, prev_ref[b], pltpu.roll(p, 1, 0))
        prev_ref[b] = p[T - 1:T, :]
        ps = p + (p_prev - p) * mu_ref[...]
        r, k, v = ps[:, 0:W], ps[:, W:2 * W], ps[:, 2 * W:3 * W]
        o3 = 3 * W
        dw = ps[:, o3:o3 + RW_DECAY_RANK]
        da = ps[:, o3 + RW_DECAY_RANK:o3 + RW_DECAY_RANK + RW_AICL_RANK]
        dg = ps[:, o3 + RW_DECAY_RANK + RW_AICL_RANK:]
        w_log = -_softplus(-(w0_ref[...] + _dot(jnp.tanh(dw), w2_ref[...]))) - 0.5
        lw = -jnp.exp(w_log)
        a = jax.nn.sigmoid(a0_ref[...] + _dot(da, a2_ref[...]))
        g = _dot(jax.nn.sigmoid(dg), g2_ref[...])
        kk = k * kk_ref[...]
        kk = kk / jnp.maximum(jnp.sqrt(_dot_hl(kk * kk, seg)), 1e-12)
        km = k * (1.0 + (a - 1.0) * ka_ref[...])
        bonus = _dot_hl(r * km * rk_ref[...], seg)
        if valid < T:
            live = trow < valid
            lw = jnp.where(live, lw, 0.0)
            kk = jnp.where(live, kk, 0.0)
            km = jnp.where(live, km, 0.0)
        cum = _dot_lh(tri_ref[...], lw)
        c_end = cum[T - 1:T, :]
        e_neg = jnp.exp(-cum)
        e_end = jnp.exp(c_end - cum)
        kka = kk * a
        return dict(A=-kk * jnp.exp(cum - lw), Bt=kka * e_neg, Kt=km * e_neg, Rt=r * jnp.exp(cum),
                    Bh=kka * e_end, Kh=km * e_end, w_end=jnp.exp(c_end), v=v, g=g, bonus=bonus)

    X_ = [prep(b) for b in range(nb)]

    ti = lax.broadcasted_iota(jnp.int32, (T, T), 0)
    tj = lax.broadcasted_iota(jnp.int32, (T, T), 1)
    strict = ti > tj
    incl = ti >= tj
    eye = jnp.where(ti == tj, 1.0, 0.0)
    chains = [(b, h) for b in range(nb) for h in range(heads)]
    C = range(len(chains))
    hs_of = [slice(h * HEAD_DIM, (h + 1) * HEAD_DIM) for _, h in chains]
    col = lambda name, c: X_[chains[c][0]][name][:, hs_of[c]]
    Ss = [S_ref[b, h] for b, h in chains]
    ARs = [jnp.concatenate([col("A", c), col("Rt", c)], axis=0) for c in C]
    BKs = [jnp.concatenate([col("Bt", c), col("Kt", c)], axis=0) for c in C]
    Gms = [_dot_nt(ARs[c], BKs[c]) for c in C]
    ARSs = [_dot_nt(ARs[c], Ss[c]) for c in C]
    Ns = [jnp.where(strict, Gm[0:T, 0:T], 0.0) for Gm in Gms]
    Ms = [jnp.where(strict, Gm[0:T, T:2 * T], 0.0) for Gm in Gms]
    Prbs = [jnp.where(incl, Gm[T:2 * T, 0:T], 0.0) for Gm in Gms]
    Prks = [jnp.where(incl, Gm[T:2 * T, T:2 * T], 0.0) for Gm in Gms]
    vhs = [col("v", c) for c in C]
    MVs = [_dot(Ms[c], vhs[c]) for c in C]
    Xs = [eye + N for N in Ns]
    Ps = [_dot(N, N) for N in Ns]
    n = 2
    while n < T:
        if 2 * n >= T:
            Xs = [Xs[c] + _dot(Xs[c], Ps[c]) for c in C]
        else:
            PXs = [_dot(jnp.concatenate([Ps[c], Xs[c]], axis=0), Ps[c]) for c in C]
            Ps = [PX[0:T] for PX in PXs]
            Xs = [Xs[c] + PXs[c][T:2 * T] for c in C]
        n *= 2
    Us = [_dot(Xs[c], ARSs[c][0:T] + MVs[c]) for c in C]
    UVs = [jnp.concatenate([Us[c], vhs[c]], axis=0) for c in C]
    Os = [ARSs[c][T:2 * T] + _dot(jnp.concatenate([Prbs[c], Prks[c]], axis=1), UVs[c]) for c in C]
    for c in C:
        b, h = chains[c]
        S_ref[b, h] = Ss[c] * col("w_end", c) + _dot_tn(UVs[c], jnp.concatenate([col("Bh", c), col("Kh", c)], axis=0))
    for c in C:
        b, h = chains[c]
        hs = hs_of[c]
        O = Os[c]
        mean = jnp.mean(O, axis=-1, keepdims=True)
        var = jnp.mean(jnp.square(O - mean), axis=-1, keepdims=True)
        on = (O - mean) * lax.rsqrt(var + RW_GN_EPS) * gnw_ref[:, hs] + gnb_ref[:, hs]
        on = on + col("bonus", c) * vhs[c]
        y_ref[b, :, hs] = on * col("g", c)

    @pl.when(li == nl - 1)
    def _():
        sfin_ref[...] = S_ref[...]


def _rwkv(p, shift0, s0, prm, seq_len, T):
    B, Lp, NP = p.shape
    heads = s0.shape[1]
    W = heads * HEAD_DIM
    nl = Lp // T
    valid = seq_len - (nl - 1) * T
    assert nl * T == Lp and 0 < valid <= T and (nl == 1 or valid == T)
    nb = RW_SEQS_PER_STEP if B % RW_SEQS_PER_STEP == 0 else 1
    tri = jnp.tril(jnp.ones((T, T), BF16))
    vec = lambda n: _const_spec((1, n))
    sds = jax.ShapeDtypeStruct
    st_spec = pl.BlockSpec((nb, heads, HEAD_DIM, HEAD_DIM), lambda b, i: (b, 0, 0, 0))
    return pl.pallas_call(
        functools.partial(_rwkv_kernel, T=T, nl=nl, valid=valid, heads=heads, nb=nb),
        grid=(B // nb, nl),
        in_specs=[pl.BlockSpec((nb, T, NP), lambda b, i: (b, i, 0)),
                  pl.BlockSpec((nb, 1, NP), lambda b, i: (b, 0, 0)), st_spec,
                  vec(NP), vec(W), _const_spec((RW_DECAY_RANK, W)), vec(W), _const_spec((RW_AICL_RANK, W)),
                  _const_spec((RW_GATE_RANK, W)), vec(W), vec(W), vec(W), vec(W), vec(W),
                  _const_spec((W, W)), _const_spec((T, T))],
        out_specs=[pl.BlockSpec((nb, T, W), lambda b, i: (b, i, 0)), st_spec],
        out_shape=[sds((B, Lp, W), F32), sds(s0.shape, F32)],
        scratch_shapes=[pltpu.VMEM((nb, heads, HEAD_DIM, HEAD_DIM), F32), pltpu.VMEM((nb, 1, NP), F32)],
        compiler_params=_params(("parallel", "arbitrary")),
        name="rwkv",
    )(p, shift0, s0, prm["mu"], prm["w0"], prm["w2"], prm["a0"], prm["a2"], prm["g2"], prm["k_k"],
      prm["k_a"], prm["r_k"], prm["gn_w"], prm["gn_b"], prm["seg"], tri)


SB_TILE = MXU_WIDTH
SB_Q_SUB = 4
SB_CAST_ROWS = 512
SB_BIAS_SPLIT = 3
LOG2E = math.log2(math.e)


def _sb_tiles(qs, kvs, carries, masks, upper):
    C, J = range(len(qs)), range(len(kvs))
    nzs = [[_dot_nt(qs[c], kvs[j][c][0]) for c in C] for j in J]
    ms = [[jnp.minimum(nzs[j][c], 0.0) for c in C] for j in J]
    ps = [[ms[j][c] - nzs[j][c] for c in C] for j in J]
    ls = [[jnp.log2(1.0 + jnp.exp2(ms[j][c] + ps[j][c])) for c in C] for j in J]
    lks = [[ms[j][c] - ls[j][c] for c in C] for j in J]
    lks = [[lks[j][c] if masks[j][c] is None else jnp.where(masks[j][c], lks[j][c], 0.0) for c in C] for j in J]
    cin = [list(carries)]
    for j in J:
        cin.append([cin[j][c] + jnp.sum(lks[j][c], axis=-1, keepdims=True) for c in C])
    css = [[_dot(lks[j][c], upper) + cin[j][c] for c in C] for j in J]
    ws = [[jnp.exp2(ps[j][c] - ls[j][c] + css[j][c]) for c in C] for j in J]
    ws = [[ws[j][c] if masks[j][c] is None else jnp.where(masks[j][c], ws[j][c], 0.0) for c in C] for j in J]
    ds = [[_dot(ws[j][c], kvs[j][c][1]) for c in C] for j in J]
    out = ds[0]
    for j in J[1:]:
        out = [out[c] + ds[j][c] for c in C]
    return out, cin[-1]


SB_K_PER_ITER = 2


def _sb_kernel(q_ref, k_ref, v_ref, bias_ref, up_ref, o_ref, kb_ref, vb_ref, *, t, nsub, scale, pair, L):
    qi = pl.program_id(2)
    heads = range(pair)
    lane = lax.broadcasted_iota(jnp.int32, (1, LANES), 1)
    is_data = lane < HEAD_DIM

    def head_lanes(x, hh):
        return x if hh == 0 else pltpu.roll(x, LANES - hh * HEAD_DIM, 1)

    @pl.when(qi == 0)
    def _():
        rows = min(SB_CAST_ROWS, L)
        ones = jnp.where(lane < HEAD_DIM + SB_BIAS_SPLIT, 1.0, 0.0)

        def cast(c, _):
            r0 = pl.multiple_of(c * rows, rows)
            kk = k_ref[0, pl.ds(r0, rows), :]
            vv = v_ref[0, pl.ds(r0, rows), :]
            for hh in heads:
                kb_ref[hh, pl.ds(r0, rows), :] = jnp.where(is_data, head_lanes(kk, hh), ones).astype(BF16)
                vb_ref[hh, pl.ds(r0, rows), :] = head_lanes(vv, hh)[:, 0:HEAD_DIM].astype(BF16)
            return 0

        lax.fori_loop(0, L // rows, cast, 0)

    q2 = q_ref[0] * (-scale * LOG2E)
    qs = [jnp.where(is_data, head_lanes(q2[s * t:(s + 1) * t, :], hh), bias_ref[0, hh:hh + 1, :]).astype(BF16)
          for s in range(nsub) for hh in heads]
    upper = up_ref[...]
    ri = lax.broadcasted_iota(jnp.int32, (t, t), 0)
    ci = lax.broadcasted_iota(jnp.int32, (t, t), 1)
    diag_mask = ci < ri

    def kv_tile(k0, chains):
        return [(kb_ref[c % pair, pl.ds(k0, t), :], vb_ref[c % pair, pl.ds(k0, t), :]) for c in chains]

    zero = jnp.zeros((t, 1), F32)
    accs = [None] * (nsub * pair)
    carries = [zero] * (nsub * pair)
    for kt in reversed(range(nsub)):
        chains = list(range(kt * pair, nsub * pair))
        k0 = pl.multiple_of((qi * nsub + kt) * t, t)
        ds, cs = _sb_tiles([qs[c] for c in chains], [kv_tile(k0, chains)], [carries[c] for c in chains],
                           [[diag_mask if c // pair == kt else None for c in chains]], upper)
        for i, c in enumerate(chains):
            accs[c] = ds[i] if accs[c] is None else accs[c] + ds[i]
            carries[c] = cs[i]

    chains = list(range(nsub * pair))
    n_tiles = qi * nsub
    kpi = SB_K_PER_ITER if nsub % SB_K_PER_ITER == 0 else 1
    no_mask = [[None] * len(chains)] * kpi

    def body(it, st):
        accs, carries = st
        k0s = [pl.multiple_of((n_tiles - 1 - it * kpi - j) * t, t) for j in range(kpi)]
        ds, carries = _sb_tiles(qs, [kv_tile(k0, chains) for k0 in k0s], list(carries), no_mask, upper)
        return tuple(a + d for a, d in zip(accs, ds)), tuple(carries)

    accs, _ = lax.fori_loop(0, n_tiles // kpi, body, (tuple(accs), tuple(carries)))
    for c in chains:
        s, hh = divmod(c, pair)
        o_ref[0, s * t:(s + 1) * t, hh * HEAD_DIM:(hh + 1) * HEAD_DIM] = accs[c]


def _sb_prompt(q, k, v, bias):
    B, L, W = q.shape
    H = W // HEAD_DIM
    pair = LANES // HEAD_DIM
    t = _row_tile(L, SB_TILE)
    nsub = SB_Q_SUB if L % (SB_Q_SUB * t) == 0 else 1
    upper = jnp.triu(jnp.ones((t, t), BF16), 1).T
    rest = -bias * LOG2E
    parts = []
    for _ in range(SB_BIAS_SPLIT):
        parts.append(rest.astype(BF16).astype(F32))
        rest = rest - parts[-1]
    bias_rows = jnp.zeros((H, LANES), F32).at[:, HEAD_DIM:HEAD_DIM + SB_BIAS_SPLIT].set(jnp.stack(parts, axis=1))
    qspec = pl.BlockSpec((1, nsub * t, LANES), lambda b, hp, i: (b, i, hp))
    kvspec = pl.BlockSpec((1, L, LANES), lambda b, hp, i: (b, 0, hp))
    return pl.pallas_call(
        functools.partial(_sb_kernel, t=t, nsub=nsub, scale=HEAD_DIM ** -0.5, pair=pair, L=L),
        grid=(B, W // LANES, L // (nsub * t)),
        in_specs=[qspec, kvspec, kvspec, pl.BlockSpec((1, pair, LANES), lambda b, hp, i: (hp, 0, 0)),
                  _const_spec((t, t))],
        out_specs=qspec,
        out_shape=jax.ShapeDtypeStruct((B, L, W), F32),
        scratch_shapes=[pltpu.VMEM((pair, L, LANES), BF16), pltpu.VMEM((pair, L, HEAD_DIM), BF16)],
        compiler_params=_params(("parallel", "parallel", "arbitrary")),
        name="sb_prompt",
    )(q, k, v, bias_rows.reshape(H // pair, pair, LANES), upper)


SB_DEC_PAGES = 8


def _sb_decode_kernel(pt_ref, q_ref, nbias_ref, *refs, pps, scale):
    k_refs, v_refs = refs[:pps], refs[pps:2 * pps]
    o_ref, acc_ref, carry_ref = refs[2 * pps:]
    step = pl.program_id(1)
    H, d, page = acc_ref.shape

    @pl.when(step == 0)
    def _():
        acc_ref[...] = jnp.zeros_like(acc_ref)
        carry_ref[...] = jnp.zeros_like(carry_ref)

    qb = jnp.broadcast_to(q_ref[0] * (-scale * LOG2E), (H, d, page))
    lane = lax.broadcasted_iota(jnp.int32, (H, 1, page), 2)
    pages = range(pps)
    nzs = [jnp.sum(k_refs[i][0, 0] * qb, axis=1, keepdims=True) + nbias_ref[...] for i in pages]
    lks = [jnp.minimum(nz, 0.0) - jnp.log2(1.0 + jnp.exp2(-jnp.abs(nz))) for nz in nzs]
    sufs = list(lks)
    sh = 1
    while sh < page:
        sufs = [x + jnp.where(lane < page - sh, pltpu.roll(x, page - sh, 2), 0.0) for x in sufs]
        sh *= 2
    carry = carry_ref[...]
    css = []
    for i in pages:
        css.append(sufs[i] - lks[i] + carry)
        carry = carry + sufs[i][:, :, 0:1]
    carry_ref[...] = carry
    ws = [jnp.exp2(lks[i] - nzs[i] + css[i]) for i in pages]
    acc = acc_ref[...]
    for i in pages:
        acc = acc + v_refs[i][0, 0] * ws[i]
    acc_ref[...] = acc

    @pl.when(step == pl.num_programs(1) - 1)
    def _():
        o_ref[0] = jnp.sum(acc, axis=-1, keepdims=True)


def _sb_decode(q, cache_k, cache_v, layer, page_table, bias):
    B, _, W = q.shape
    n_pages = page_table.shape[1]
    depth, n_pool, page, H, d = cache_k.shape
    pps = SB_DEC_PAGES if n_pages % SB_DEC_PAGES == 0 else 1
    ck = jnp.transpose(cache_k, (0, 1, 3, 4, 2))
    cv = jnp.transpose(cache_v, (0, 1, 3, 4, 2))
    nbias = jnp.broadcast_to((-bias * LOG2E)[:, None, None], (H, 1, page))
    kv = [pl.BlockSpec((1, 1, H, d, page),
                       lambda b, s, pt, i=i: (layer, pt[b, n_pages - 1 - (s * pps + i)], 0, 0, 0))
          for i in range(pps)]
    qo = pl.BlockSpec((1, H, d, 1), lambda b, s, pt: (b, 0, 0, 0))
    out = pl.pallas_call(
        functools.partial(_sb_decode_kernel, pps=pps, scale=d ** -0.5),
        grid_spec=pltpu.PrefetchScalarGridSpec(
            num_scalar_prefetch=1,
            grid=(B, n_pages // pps),
            in_specs=[qo, pl.BlockSpec((H, 1, page), lambda b, s, pt: (0, 0, 0))] + kv + kv,
            out_specs=qo,
            scratch_shapes=[pltpu.VMEM((H, d, page), F32), pltpu.VMEM((H, 1, page), F32)]),
        out_shape=jax.ShapeDtypeStruct((B, H, d, 1), F32),
        compiler_params=_params(("parallel", "arbitrary")),
        name="sb_decode",
    )(page_table, q.reshape(B, H, d, 1), nbias, *([ck] * pps), *([cv] * pps))
    return out.reshape(B, 1, W)


def _block_diag(m):
    G, a, b = m.shape
    per = LANES // min(a, b)
    eye = jnp.eye(per, dtype=m.dtype)
    mt = m.reshape(G // per, per, a, b)
    return (mt[:, :, :, None, :] * eye[None, :, None, :, None]).reshape(G // per, per * a, per * b)


def _layer(x, mod, P, seq_len, s5_0, rw_s0, rw_shift0, sb_fn, norm_f):
    G, R, D = x.shape
    m = lambda i, j: mod[:, :, i, j, :]
    x = _ffn(x, m(0, 0), m(0, 1), m(0, 2), P["norm_g"][0], P["w_up"][0], P["w_down"][0])
    u, p, q, k, v, gates = _inproj(x, m(1, 0), m(1, 1), P["norm_g"][1], P["w_in"], P["widths"])
    B = G * R // seq_len
    seqs = lambda t: t.reshape(B, seq_len, t.shape[-1])
    pad = (-seq_len) % SUBLANES

    def padded(t):
        t = seqs(t)
        return jnp.pad(t, ((0, 0), (0, pad), (0, 0))) if pad else t

    y_s5, s5_re, s5_im = _s5(padded(u), s5_0[0], s5_0[1], P["s5"], seq_len)
    T = 64 if seq_len % 64 == 0 else SUBLANES
    y_rw, rw_s = _rwkv(padded(p), rw_shift0, rw_s0, P["rw"], seq_len, T)
    y_sb = sb_fn(seqs(q), seqs(k), seqs(v))
    rows = lambda t: t[:, :seq_len].reshape(G, R, t.shape[-1])
    x = _merge(x, m(1, 2), rows(y_s5), rows(y_rw), rows(y_sb), gates, P["w_branch"], P["w_out"])
    x = _ffn(x, m(2, 0), m(2, 1), m(2, 2), P["norm_g"][2], P["w_up"][1], P["w_down"][1], norm_f)
    return x, (seqs(k), seqs(v), s5_re, s5_im, rw_s, seqs(p)[:, -1])


def kernel(x_prompt, x_sample, cache_k, cache_v, state_s5_re, state_s5_im, state_rwkv, state_rwkv_shift, page_table, c_prompt, c_sample, norm_g, w_ada, b_ada, w_ffn_up, w_ffn_down, w_in, s5_lam_re, s5_lam_im, s5_log_dt, s5_b_re, s5_b_im, s5_c_re, s5_c_im, s5_d, s5_w_glu, s5_b_glu, rw_mu, rw_w0, rw_w2, rw_a0, rw_a2, rw_g2, rw_k_k, rw_k_a, rw_r_k, rw_gn_w, rw_gn_b, sb_bias, w_branch, w_out, norm_f):
    bp, L, D = x_prompt.shape
    bs = x_sample.shape[0]
    depth = w_in.shape[0]
    G5, P5 = s5_lam_re.shape[1:]
    n_state = G5 * P5
    s5_w = G5 * S5_GROUP
    heads = state_rwkv.shape[2]
    W = heads * HEAD_DIM
    n_proj = state_rwkv_shift.shape[-1]
    widths = (s5_w, n_proj, W, W, W, 3 * D)

    c_all = jnp.concatenate([c_prompt, c_sample], axis=0)
    c_all = jnp.pad(c_all, ((0, (-c_all.shape[0]) % SUBLANES), (0, 0)))
    mod_all = _ada(c_all, w_ada, b_ada)
    mod_p = mod_all[:, :bp].reshape(depth, bp, 1, 3, 3, D)
    mod_s = mod_all[:, bp:bp + bs].reshape(depth, 1, bs, 3, 3, D)

    seg = (jnp.arange(W)[:, None] // HEAD_DIM == jnp.arange(W)[None, :] // HEAD_DIM).astype(BF16)
    xp = x_prompt
    xs = x_sample.reshape(1, bs, D)
    outs_p, outs_s = [], []
    for l in range(depth):
        bb_re, bb_im, pw_re, pw_im = _s5_params(s5_lam_re[l], s5_lam_im[l], s5_log_dt[l], s5_b_re[l], s5_b_im[l])
        P = dict(
            norm_g=norm_g[l], w_up=w_ffn_up[l].astype(BF16), w_down=w_ffn_down[l].astype(BF16),
            w_in=w_in[l].astype(BF16), widths=widths,
            w_branch=w_branch[l].astype(BF16), w_out=w_out[l].astype(BF16),
            s5=dict(bb_re=_block_diag(bb_re).astype(BF16), bb_im=_block_diag(bb_im).astype(BF16),
                    c_re=_block_diag(jnp.swapaxes(s5_c_re[l], 1, 2)).astype(BF16),
                    c_im=_block_diag(jnp.swapaxes(s5_c_im[l], 1, 2)).astype(BF16),
                    d=s5_d[l].reshape(1, s5_w), w_glu=s5_w_glu[l].astype(BF16), b_glu=s5_b_glu[l].reshape(1, s5_w),
                    pw_re=pw_re.reshape(S5_POWERS, n_state), pw_im=pw_im.reshape(S5_POWERS, n_state)),
            rw=dict(mu=rw_mu[l].reshape(1, n_proj), w0=rw_w0[l].reshape(1, W), w2=rw_w2[l].astype(BF16),
                    a0=rw_a0[l].reshape(1, W), a2=rw_a2[l].astype(BF16), g2=rw_g2[l].astype(BF16),
                    k_k=rw_k_k[l].reshape(1, W), k_a=rw_k_a[l].reshape(1, W), r_k=rw_r_k[l].reshape(1, W),
                    gn_w=rw_gn_w[l].reshape(1, W), gn_b=rw_gn_b[l].reshape(1, W), seg=seg))
        nf = norm_f if l == depth - 1 else None
        bias = sb_bias[l]
        zs5 = jnp.zeros((bp, 1, n_state), F32)
        xp, st = _layer(xp, mod_p[l], P, L, (zs5, zs5), jnp.zeros((bp, heads, HEAD_DIM, HEAD_DIM), F32),
                        jnp.zeros((bp, 1, n_proj), F32),
                        lambda q, k, v: _sb_prompt(q, k, v, bias), nf)
        outs_p.append(st)
        xs, st = _layer(xs, mod_s[l], P, 1,
                        (state_s5_re[l].reshape(bs, 1, n_state), state_s5_im[l].reshape(bs, 1, n_state)),
                        state_rwkv[l], state_rwkv_shift[l].reshape(bs, 1, n_proj),
                        lambda q, k, v: _sb_decode(q, cache_k, cache_v, l, page_table, bias), nf)
        outs_s.append(st)

    def stacked(outs, i, shape):
        return jnp.stack([o[i] for o in outs]).reshape((depth,) + shape)

    def group(outs, b, seq):
        return (stacked(outs, 0, (b, seq, heads, HEAD_DIM)), stacked(outs, 1, (b, seq, heads, HEAD_DIM)),
                stacked(outs, 2, (b, G5, P5)), stacked(outs, 3, (b, G5, P5)),
                stacked(outs, 4, (b, heads, HEAD_DIM, HEAD_DIM)), stacked(outs, 5, (b, n_proj)))

    return (xp, xs.reshape(bs, 1, D)) + group(outs_p, bp, L) + group(outs_s, bs, 1)
```
